```python
import jax, jax.numpy as jnp
from jax import lax
import numpy as np

D_MODEL = 2048
BATCH = 8
SEQ = 2048
DEPTH = 2

GRID_W = 64
CTX_LEN = 256
HEAD_DIM = 128
AXIS_DIM = HEAD_DIM // 2
ROPE_THETA = 10000.0
Q_BLOCK = 128
A_HEADS = 6
A_KV = 2
B_HEADS = 6
B_KV = 2
WINDOW = 128
C_GROUPS = 4
C_GROUP_DIM = 128
CHUNK = 128
A_WIDTH = A_HEADS * HEAD_DIM
B_WIDTH = B_HEADS * HEAD_DIM
C_WIDTH = C_GROUPS * C_GROUP_DIM
MIX_WIDTH = A_WIDTH + B_WIDTH + C_WIDTH
A_KV_WIDTH = A_KV * HEAD_DIM
B_KV_WIDTH = B_KV * HEAD_DIM
PROJ_SPLITS = [
    A_WIDTH,
    A_WIDTH + A_KV_WIDTH,
    A_WIDTH + 2 * A_KV_WIDTH,
    A_WIDTH + 2 * A_KV_WIDTH + B_WIDTH,
    A_WIDTH + 2 * A_KV_WIDTH + B_WIDTH + B_KV_WIDTH,
    A_WIDTH + 2 * A_KV_WIDTH + B_WIDTH + 2 * B_KV_WIDTH,
    A_WIDTH + 2 * A_KV_WIDTH + B_WIDTH + 2 * B_KV_WIDTH + C_WIDTH,
]
IN_WIDTH = A_WIDTH + 2 * A_KV_WIDTH + B_WIDTH + 2 * B_KV_WIDTH + 2 * C_WIDTH
PEER_HEADS = 8
PEER_KEYS = 128
PEER_EXPERTS = PEER_KEYS * PEER_KEYS
PEER_QDIM = 256
PEER_HALF = PEER_QDIM // 2
PEER_TOPK = 16
TOKEN_BLOCK = 128
N_MOD = 6
EPS = 1e-6
NEG_INF = -1e30
F32 = jnp.float32

kernel_name = "hybrid_dit_gqa_swa_gmlp_peer"


def rms_unit(x):
    xf = x.astype(F32)
    return (xf * lax.rsqrt(jnp.mean(jnp.square(xf), -1, keepdims=True) + EPS)).astype(x.dtype)


def rmsnorm(x, g):
    return rms_unit(x) * g


def modulate(h, g, shift, scale):
    return rmsnorm(h, g) * (1.0 + scale) + shift


def axial_rope(rows):
    t = jnp.arange(rows * GRID_W)
    row = (t // GRID_W).astype(F32)
    col = (t % GRID_W).astype(F32)
    inv = ROPE_THETA ** (-jnp.arange(AXIS_DIM // 2, dtype=F32) / (AXIS_DIM // 2))
    ar, ac = row[:, None] * inv, col[:, None] * inv
    ang = jnp.concatenate([ar, ar, ac, ac], -1)
    return jnp.cos(ang), jnp.sin(ang)


def apply_rope(x, rope):
    cos, sin = rope
    xf = x.astype(F32)
    x1, x2, x3, x4 = jnp.split(xf, 4, -1)
    rot = jnp.concatenate([-x2, x1, -x4, x3], -1)
    return (xf * cos[None, :, None, :] + rot * sin[None, :, None, :]).astype(x.dtype)


def prep_heads(t, n_heads, gain, rope):
    b, l, _ = t.shape
    t = rmsnorm(t.reshape(b, l, n_heads, HEAD_DIM), gain)
    return t if rope is None else apply_rope(t, rope)


def attend(q, k, v, sink=None):
    s = jnp.einsum('bqkgd,btkd->bkgqt', q, k).astype(F32) * (HEAD_DIM ** -0.5)
    if sink is not None:
        sk = jnp.broadcast_to(sink.astype(F32)[None, :, :, None, None], s.shape[:-1] + (1,))
        s = jnp.concatenate([s, sk], -1)
    p = jax.nn.softmax(s, -1)[..., :k.shape[1]].astype(v.dtype)
    return jnp.einsum('bkgqt,btkd->bqkgd', p, v)


def global_attention(q, k_all, v_all):
    b, l = q.shape[:2]
    nblk = l // Q_BLOCK
    qb = q.reshape(b, nblk, Q_BLOCK, A_KV, A_HEADS // A_KV, HEAD_DIM).transpose(1, 0, 2, 3, 4, 5)
    out = lax.map(lambda blk: attend(blk, k_all, v_all), qb)
    return out.transpose(1, 0, 2, 3, 4, 5).reshape(b, l, A_WIDTH)


def window_attention(q, k, v, k_ctx, v_ctx, sink):
    b, l = q.shape[:2]
    nblk = l // Q_BLOCK
    g = B_HEADS // B_KV
    qb = q.reshape(b, nblk, Q_BLOCK, B_KV, g, HEAD_DIM)

    def band(t):
        tp = jnp.pad(t, ((0, 0), (Q_BLOCK, Q_BLOCK), (0, 0), (0, 0))).reshape(b, nblk + 2, Q_BLOCK, B_KV, HEAD_DIM)
        return jnp.concatenate([tp[:, :-2], tp[:, 1:-1], tp[:, 2:]], axis=2)

    kw, vw = band(k), band(v)
    blk = jnp.arange(nblk)[:, None, None] * Q_BLOCK
    qpos = blk + jnp.arange(Q_BLOCK)[None, :, None]
    kpos = blk - Q_BLOCK + jnp.arange(3 * Q_BLOCK)[None, None, :]
    valid = (jnp.abs(kpos - qpos) <= WINDOW) & (kpos >= 0) & (kpos < l)
    scale = HEAD_DIM ** -0.5
    s_w = jnp.einsum('bnqkgd,bnskd->bnkgqs', qb, kw).astype(F32) * scale
    s_w = jnp.where(valid[None, :, None, None], s_w, NEG_INF)
    s_c = jnp.einsum('bnqkgd,bskd->bnkgqs', qb, k_ctx).astype(F32) * scale
    s_s = jnp.broadcast_to(sink.astype(F32).reshape(1, 1, B_KV, g, 1, 1), s_c.shape[:-1] + (1,))
    p = jax.nn.softmax(jnp.concatenate([s_w, s_c, s_s], -1), -1).astype(v.dtype)
    nw, nc = 3 * Q_BLOCK, k_ctx.shape[1]
    o = (jnp.einsum('bnkgqs,bnskd->bnqkgd', p[..., :nw], vw)
         + jnp.einsum('bnkgqs,bskd->bnqkgd', p[..., nw:nw + nc], v_ctx))
    return o.reshape(b, l, B_WIDTH)


def chunk_mlp(u, v, vn_g, ws, bs):
    b, l, _ = u.shape
    n = l // CHUNK
    u = jax.nn.gelu(u).reshape(b, n, CHUNK, C_GROUPS, C_GROUP_DIM)
    v = rmsnorm(jax.nn.gelu(v).reshape(b, l, C_GROUPS, C_GROUP_DIM), vn_g.reshape(C_GROUPS, C_GROUP_DIM))
    v = v.reshape(b, n, CHUNK, C_GROUPS, C_GROUP_DIM)
    s = jnp.einsum('gpq,bnqgc->bnpgc', ws, v) + bs.T[None, None, :, :, None]
    return (u * s).reshape(b, l, C_WIDTH)


def merge(ya, yb, yc, g, w_out):
    y = jnp.concatenate([rms_unit(ya), rms_unit(yb), rms_unit(yc)], -1) * g
    return y @ w_out


def peer_ffn(h, wq, keys, u_tab, v_tab):
    b, l, d = h.shape
    n_tok = b * l
    t = h.reshape(n_tok, d)
    q = (t @ wq).reshape(n_tok, PEER_HEADS, 2, PEER_HALF).astype(F32)
    s = jnp.einsum('thpc,hpkc->thpk', q, keys.astype(F32))
    v1, i1 = lax.top_k(s[:, :, 0], PEER_TOPK)
    v2, i2 = lax.top_k(s[:, :, 1], PEER_TOPK)
    cand = (v1[..., :, None] + v2[..., None, :]).reshape(n_tok, PEER_HEADS, PEER_TOPK * PEER_TOPK)
    vals, ci = lax.top_k(cand, PEER_TOPK)
    e = (jnp.take_along_axis(i1, ci // PEER_TOPK, -1) * PEER_KEYS
         + jnp.take_along_axis(i2, ci % PEER_TOPK, -1))
    g = jax.nn.softmax(vals, -1).astype(h.dtype)
    nb = n_tok // TOKEN_BLOCK

    def block(args):
        tb, eb, gb = args
        hid = jnp.einsum('td,tkd->tk', tb, u_tab[eb])
        return jnp.einsum('tk,tkd->td', jax.nn.gelu(hid) * gb, v_tab[eb])

    out = lax.map(block, (t.reshape(nb, TOKEN_BLOCK, d),
                          e.reshape(nb, TOKEN_BLOCK, PEER_HEADS * PEER_TOPK),
                          g.reshape(nb, TOKEN_BLOCK, PEER_HEADS * PEER_TOPK)))
    return out.reshape(b, l, d)


def setup_inputs(seed: int = 0) -> dict:
    key = jax.random.key(seed)
    ks = jax.random.split(key, 24)
    D, L = D_MODEL, DEPTH

    def nrm(k, shape, s):
        return jax.random.normal(k, shape, F32) * s

    return {
        "x": nrm(ks[0], (BATCH, SEQ, D), 1.0),
        "c": nrm(ks[1], (BATCH, D), 1.0),
        "ctx": nrm(ks[2], (BATCH, CTX_LEN, D), 1.0),
        "c_ctx": nrm(ks[3], (D,), 1.0),
        "w_ada": nrm(ks[4], (L, D, N_MOD * D), D ** -0.5),
        "b_ada": nrm(ks[5], (L, N_MOD * D), 0.01),
        "norm1_g": 1.0 + nrm(ks[6], (L, D), 0.01),
        "norm2_g": 1.0 + nrm(ks[7], (L, D), 0.01),
        "w_in": nrm(ks[8], (L, D, IN_WIDTH), D ** -0.5),
        "qn_a": 1.0 + nrm(ks[9], (L, HEAD_DIM), 0.01),
        "kn_a": 1.0 + nrm(ks[10], (L, HEAD_DIM), 0.01),
        "qn_b": 1.0 + nrm(ks[11], (L, HEAD_DIM), 0.01),
        "kn_b": 1.0 + nrm(ks[12], (L, HEAD_DIM), 0.01),
        "sink_b": nrm(ks[13], (L, B_HEADS), 0.5),
        "vn_c": 1.0 + nrm(ks[14], (L, C_WIDTH), 0.01),
        "ws_c": nrm(ks[15], (L, C_GROUPS, CHUNK, CHUNK), CHUNK ** -0.5),
        "bs_c": 1.0 + nrm(ks[16], (L, C_GROUPS, CHUNK), 0.01),
        "g_group": 1.0 + nrm(ks[17], (L, MIX_WIDTH), 0.01),
        "w_out": nrm(ks[18], (L, MIX_WIDTH, D), MIX_WIDTH ** -0.5),
        "peer_wq": nrm(ks[19], (L, D, PEER_HEADS * PEER_QDIM), D ** -0.5),
        "peer_keys": nrm(ks[20], (L, PEER_HEADS, 2, PEER_KEYS, PEER_HALF), PEER_HALF ** -0.5),
        "peer_u": nrm(ks[21], (L, PEER_EXPERTS, D), D ** -0.5),
        "peer_v": nrm(ks[22], (L, PEER_EXPERTS, D), 0.5),
    }


def reference(x, c, ctx, c_ctx, w_ada, b_ada, norm1_g, norm2_g, w_in, qn_a, kn_a, qn_b, kn_b,
              sink_b, vn_c, ws_c, bs_c, g_group, w_out, peer_wq, peer_keys, peer_u, peer_v):
    b = x.shape[0]
    rows = x.shape[1] // GRID_W
    rope = axial_rope(rows)
    n_ctx = ctx.shape[1]
    ga = A_HEADS // A_KV
    gb = B_HEADS // B_KV
    h, hc = x, ctx
    for l in range(DEPTH):
        last = l == DEPTH - 1
        ml = [m[:, None, :] for m in jnp.split(jax.nn.silu(c) @ w_ada[l] + b_ada[l], N_MOD, -1)]
        mc = jnp.split(jax.nn.silu(c_ctx) @ w_ada[l] + b_ada[l], N_MOD, -1)
        xn = modulate(h, norm1_g[l], ml[0], ml[1])
        cn = modulate(hc, norm1_g[l], mc[0], mc[1])
        qa, ka, va, qb, kb, vb, uc, vc = jnp.split(xn @ w_in[l], PROJ_SPLITS, -1)
        cqa, cka, cva, cqb, ckb, cvb, cuc, cvc = jnp.split(cn @ w_in[l], PROJ_SPLITS, -1)
        cka_h = prep_heads(cka, A_KV, kn_a[l], None)
        cva_h = cva.reshape(b, n_ctx, A_KV, HEAD_DIM)
        ckb_h = prep_heads(ckb, B_KV, kn_b[l], None)
        cvb_h = cvb.reshape(b, n_ctx, B_KV, HEAD_DIM)
        qa_h = prep_heads(qa, A_HEADS, qn_a[l], rope)
        ka_h = prep_heads(ka, A_KV, kn_a[l], rope)
        va_h = va.reshape(b, -1, A_KV, HEAD_DIM)
        ya = global_attention(qa_h, jnp.concatenate([cka_h, ka_h], 1), jnp.concatenate([cva_h, va_h], 1))
        qb_h = prep_heads(qb, B_HEADS, qn_b[l], rope)
        kb_h = prep_heads(kb, B_KV, kn_b[l], rope)
        vb_h = vb.reshape(b, -1, B_KV, HEAD_DIM)
        yb = window_attention(qb_h, kb_h, vb_h, ckb_h, cvb_h, sink_b[l])
        yc = chunk_mlp(uc, vc, vn_c[l], ws_c[l], bs_c[l])
        h = h + ml[2] * merge(ya, yb, yc, g_group[l], w_out[l])
        h = h + ml[5] * peer_ffn(modulate(h, norm2_g[l], ml[3], ml[4]), peer_wq[l], peer_keys[l], peer_u[l], peer_v[l])
        if not last:
            cqa_h = prep_heads(cqa, A_HEADS, qn_a[l], None).reshape(b, n_ctx, A_KV, ga, HEAD_DIM)
            cya = attend(cqa_h, cka_h, cva_h).reshape(b, n_ctx, A_WIDTH)
            cqb_h = prep_heads(cqb, B_HEADS, qn_b[l], None).reshape(b, n_ctx, B_KV, gb, HEAD_DIM)
            cyb = attend(cqb_h, ckb_h, cvb_h, sink_b[l].reshape(B_KV, gb)).reshape(b, n_ctx, B_WIDTH)
            cyc = chunk_mlp(cuc, cvc, vn_c[l], ws_c[l], bs_c[l])
            hc = hc + mc[2] * merge(cya, cyb, cyc, g_group[l], w_out[l])
            hc = hc + mc[5] * peer_ffn(modulate(hc, norm2_g[l], mc[3], mc[4]), peer_wq[l], peer_keys[l], peer_u[l], peer_v[l])
    return h
```

```python
import functools

import jax
import jax.numpy as jnp
from jax import lax
from jax.experimental import pallas as pl
from jax.experimental.pallas import tpu as pltpu

F32 = jnp.float32
BF16 = jnp.bfloat16

GRID_W = 64
HEAD_DIM = 128
AXIS_DIM = HEAD_DIM // 2
ROPE_THETA = 10000.0
A_HEADS, A_KV = 6, 2
B_HEADS, B_KV = 6, 2
WINDOW = 128
C_GROUPS, C_GROUP_DIM, CHUNK = 4, 128, 128
A_WIDTH = A_HEADS * HEAD_DIM
B_WIDTH = B_HEADS * HEAD_DIM
C_WIDTH = C_GROUPS * C_GROUP_DIM
MIX_WIDTH = A_WIDTH + B_WIDTH + C_WIDTH
A_KV_WIDTH = A_KV * HEAD_DIM
B_KV_WIDTH = B_KV * HEAD_DIM
IN_WIDTH = A_WIDTH + 2 * A_KV_WIDTH + B_WIDTH + 2 * B_KV_WIDTH + 2 * C_WIDTH
GROUP = A_HEADS // A_KV
PEER_HEADS = 8
PEER_KEYS = 128
PEER_QDIM = 256
PEER_HALF = PEER_QDIM // 2
PEER_TOPK = 16
N_MOD = 6
EPS = 1e-6
NEG_INF = -1e30

OFF_QA = 0
OFF_KA = OFF_QA + A_WIDTH
OFF_VA = OFF_KA + A_KV_WIDTH
OFF_QB = OFF_VA + A_KV_WIDTH
OFF_KB = OFF_QB + B_WIDTH
OFF_VB = OFF_KB + B_KV_WIDTH
OFF_UC = OFF_VB + B_KV_WIDTH
OFF_VC = OFF_UC + C_WIDTH

MOD_ROWS = 16
TOKEN_TILE = 512
IN_COL_TILE = 256
OUT_COL_TILE = 512
ADA_COL_TILE = 1024
Q_TILE_GLOBAL = 256
Q_TILE_WINDOW = 128
EXPERT_TILE = 1024
LANE = 128
SUBLANE = 8
VMEM_LIMIT = 56 * 1024 * 1024


def _tile(n, preferred):
    t = min(preferred, n)
    while n % t:
        t -= LANE
    return t


def _cparams(sem):
    return pltpu.CompilerParams(dimension_semantics=sem, vmem_limit_bytes=VMEM_LIMIT)


def _rms_scale(x):
    return lax.rsqrt(jnp.mean(x * x, -1, keepdims=True) + EPS)


def _ada_kernel(c_ref, w_ref, b_ref, o_ref):
    c = c_ref[...]
    a = (c * jax.nn.sigmoid(c)).astype(BF16)
    o_ref[0] = jnp.dot(a, w_ref[0].astype(BF16), preferred_element_type=F32) + b_ref[0]


def _ada_modulation(cc, w_ada, b_ada):
    depth, d, n = w_ada.shape
    tn = _tile(n, ADA_COL_TILE)
    return pl.pallas_call(
        _ada_kernel,
        grid=(depth, n // tn),
        in_specs=[
            pl.BlockSpec((MOD_ROWS, d), lambda l, j: (0, 0)),
            pl.BlockSpec((1, d, tn), lambda l, j: (l, 0, j)),
            pl.BlockSpec((1, 1, tn), lambda l, j: (l, 0, j)),
        ],
        out_specs=pl.BlockSpec((1, MOD_ROWS, tn), lambda l, j: (l, 0, j)),
        out_shape=jax.ShapeDtypeStruct((depth, MOD_ROWS, n), F32),
        compiler_params=_cparams(("arbitrary", "arbitrary")),
        name="ada_modulation",
    )(cc, w_ada, b_ada.reshape(depth, 1, n))


def _col_tile_kinds():
    kinds = []
    for j in range(IN_WIDTH // IN_COL_TILE):
        c = j * IN_COL_TILE
        if c < OFF_VA or OFF_QB <= c < OFF_VB:
            kinds.append("head")
        elif c < OFF_QB or c < OFF_UC:
            kinds.append("plain")
        elif c < OFF_VC:
            kinds.append("gelu")
        else:
            kinds.append("gelu_norm")
    return kinds


def _kind_predicate(j, kinds, kind):
    pred = None
    for idx, k in enumerate(kinds):
        if k == kind:
            p = j == idx
            pred = p if pred is None else jnp.logical_or(pred, p)
    return pred


def _inproj_kernel(h_ref, mod_ref, g_ref, w_ref, gain_ref, cos_ref, sin_ref, o_ref, xn_ref, *, use_rope):
    j = pl.program_id(1)

    @pl.when(j == 0)
    def _():
        x = h_ref[...]
        xn = x * _rms_scale(x) * g_ref[...]
        xn = xn * (1.0 + mod_ref[0, 1:2, :]) + mod_ref[0, 0:1, :]
        xn_ref[...] = xn.astype(BF16)

    acc = jnp.dot(xn_ref[...], w_ref[...], preferred_element_type=F32)
    gain = gain_ref[...]
    halves = [(acc[:, k * LANE:(k + 1) * LANE], gain[:, k * LANE:(k + 1) * LANE])
              for k in range(IN_COL_TILE // LANE)]

    def store(parts):
        o_ref[...] = jnp.concatenate(parts, 1).astype(o_ref.dtype)

    def rope(a):
        if not use_rope:
            return a
        lane = lax.broadcasted_iota(jnp.int32, a.shape, 1)
        first = (lane // (AXIS_DIM // 2)) % 2 == 0
        rot = jnp.where(first, pltpu.roll(a, HEAD_DIM - AXIS_DIM // 2, 1), pltpu.roll(a, AXIS_DIM // 2, 1))
        return a * cos_ref[...] + rot * sin_ref[...]

    kinds = _col_tile_kinds()

    @pl.when(_kind_predicate(j, kinds, "head"))
    def _():
        store([rope(a * _rms_scale(a) * g) for a, g in halves])

    @pl.when(_kind_predicate(j, kinds, "plain"))
    def _():
        store([a for a, _ in halves])

    @pl.when(_kind_predicate(j, kinds, "gelu"))
    def _():
        store([jax.nn.gelu(a) for a, _ in halves])

    @pl.when(_kind_predicate(j, kinds, "gelu_norm"))
    def _():
        def f(a, g):
            a = jax.nn.gelu(a)
            return a * _rms_scale(a) * g
        store([f(a, g) for a, g in halves])


def _in_projection(h, mod, mod_row_map, norm_g, w_in, gains, cos, sin_signed, *, tiles_per_seq, use_rope):
    m, d = h.shape
    tm = _tile(m, TOKEN_TILE)
    if use_rope:
        pos_map = lambda i, j: (i % tiles_per_seq, 0)
    else:
        pos_map = lambda i, j: (0, 0)
    return pl.pallas_call(
        functools.partial(_inproj_kernel, use_rope=use_rope),
        grid=(m // tm, IN_WIDTH // IN_COL_TILE),
        in_specs=[
            pl.BlockSpec((tm, d), lambda i, j: (i, 0)),
            pl.BlockSpec((1, N_MOD, d), lambda i, j: (mod_row_map(i), 0, 0)),
            pl.BlockSpec((1, d), lambda i, j: (0, 0)),
            pl.BlockSpec((d, IN_COL_TILE), lambda i, j: (0, j)),
            pl.BlockSpec((1, IN_COL_TILE), lambda i, j: (0, j)),
            pl.BlockSpec((tm, HEAD_DIM), pos_map),
            pl.BlockSpec((tm, HEAD_DIM), pos_map),
        ],
        out_specs=pl.BlockSpec((tm, IN_COL_TILE), lambda i, j: (i, j)),
        out_shape=jax.ShapeDtypeStruct((m, IN_WIDTH), BF16),
        scratch_shapes=[pltpu.VMEM((tm, d), BF16)],
        compiler_params=_cparams(("arbitrary", "arbitrary")),
        name="in_projection_rope" if use_rope else "in_projection_ctx",
    )(h, mod, norm_g, w_in, gains, cos, sin_signed)


_NT = (((1,), (1,)), ((), ()))


def _stack_heads(q_refs):
    return jnp.concatenate([r[...] for r in q_refs], 0)


def _q_specs(rows, row_fn, off_q):
    def spec(g):
        return pl.BlockSpec((rows, HEAD_DIM),
                            lambda *ids: (row_fn(*ids), off_q // HEAD_DIM + ids[1] * GROUP + g))
    return [spec(g) for g in range(GROUP)]


def _unstack_heads(o, tq):
    return jnp.concatenate([o[g * tq:(g + 1) * tq] for g in range(GROUP)], 1)


def _sink_column(sink_ref, kv, tq):
    row = lax.broadcasted_iota(jnp.int32, (GROUP * tq, 1), 0)
    col = jnp.full((GROUP * tq, 1), sink_ref[kv * GROUP + GROUP - 1], F32)
    for g in range(GROUP - 2, -1, -1):
        col = jnp.where(row < (g + 1) * tq, sink_ref[kv * GROUP + g], col)
    return col


def _attn_global_kernel(q0_ref, q1_ref, q2_ref, kc_ref, vc_ref, kl_ref, vl_ref, o_ref):
    tq = q0_ref.shape[0]
    qs = _stack_heads((q0_ref, q1_ref, q2_ref))
    sc = lax.dot_general(qs, kc_ref[...], _NT, preferred_element_type=F32)
    sl = lax.dot_general(qs, kl_ref[...], _NT, preferred_element_type=F32)
    m = jnp.maximum(jnp.max(sc, -1, keepdims=True), jnp.max(sl, -1, keepdims=True))
    pc = jnp.exp(sc - m)
    pw = jnp.exp(sl - m)
    denom = jnp.sum(pc, -1, keepdims=True) + jnp.sum(pw, -1, keepdims=True)
    o = (jnp.dot(pc.astype(BF16), vc_ref[...], preferred_element_type=F32)
         + jnp.dot(pw.astype(BF16), vl_ref[...], preferred_element_type=F32))
    o_ref[...] = _unstack_heads(o / denom, tq).astype(o_ref.dtype)


def _global_attention(proj, proj_ctx, batch, seq, n_ctx):
    tq = _tile(seq, Q_TILE_GLOBAL)
    nq = seq // tq
    qw = GROUP * HEAD_DIM
    return pl.pallas_call(
        _attn_global_kernel,
        grid=(batch, A_KV, nq),
        in_specs=[
            *_q_specs(tq, lambda b, k, i: b * nq + i, OFF_QA),
            pl.BlockSpec((n_ctx, HEAD_DIM), lambda b, k, i: (b, OFF_KA // HEAD_DIM + k)),
            pl.BlockSpec((n_ctx, HEAD_DIM), lambda b, k, i: (b, OFF_VA // HEAD_DIM + k)),
            pl.BlockSpec((seq, HEAD_DIM), lambda b, k, i: (b, OFF_KA // HEAD_DIM + k)),
            pl.BlockSpec((seq, HEAD_DIM), lambda b, k, i: (b, OFF_VA // HEAD_DIM + k)),
        ],
        out_specs=pl.BlockSpec((tq, qw), lambda b, k, i: (b * nq + i, k)),
        out_shape=jax.ShapeDtypeStruct((batch * seq, A_WIDTH), BF16),
        compiler_params=_cparams(("arbitrary", "arbitrary", "arbitrary")),
        name="attn_global",
    )(proj, proj, proj, proj_ctx, proj_ctx, proj, proj)


def _attn_window_kernel(sink_ref, q0_ref, q1_ref, q2_ref, kc_ref, vc_ref, kl_ref, vl_ref, o_ref):
    tq = q0_ref.shape[0]
    seq = kl_ref.shape[0]
    span = tq + 2 * WINDOW
    kv = pl.program_id(1)
    i = pl.program_id(2)
    start = pl.multiple_of(jnp.clip(i * tq - WINDOW, 0, seq - span), LANE)
    qs = _stack_heads((q0_ref, q1_ref, q2_ref))
    sw = lax.dot_general(qs, kl_ref[pl.ds(start, span), :], _NT, preferred_element_type=F32)
    qpos = i * tq + lax.broadcasted_iota(jnp.int32, (tq, span), 0)
    kpos = start + lax.broadcasted_iota(jnp.int32, (tq, span), 1)
    valid = jnp.abs(kpos - qpos) <= WINDOW
    sw = jnp.where(jnp.concatenate([valid] * GROUP, 0), sw, NEG_INF)
    sc = lax.dot_general(qs, kc_ref[...], _NT, preferred_element_type=F32)
    ss = _sink_column(sink_ref, kv, tq)
    m = jnp.maximum(jnp.maximum(jnp.max(sw, -1, keepdims=True), jnp.max(sc, -1, keepdims=True)), ss)
    pw = jnp.exp(sw - m)
    pc = jnp.exp(sc - m)
    denom = jnp.sum(pw, -1, keepdims=True) + jnp.sum(pc, -1, keepdims=True) + jnp.exp(ss - m)
    o = (jnp.dot(pw.astype(BF16), vl_ref[pl.ds(start, span), :], preferred_element_type=F32)
         + jnp.dot(pc.astype(BF16), vc_ref[...], preferred_element_type=F32))
    o_ref[...] = _unstack_heads(o / denom, tq).astype(o_ref.dtype)


def _window_attention(sink, proj, proj_ctx, batch, seq, n_ctx):
    tq = Q_TILE_WINDOW
    nq = seq // tq
    qw = GROUP * HEAD_DIM
    return pl.pallas_call(
        _attn_window_kernel,
        grid=(batch, B_KV, nq),
        in_specs=[
            pl.BlockSpec(memory_space=pltpu.SMEM),
            *_q_specs(tq, lambda b, k, i: b * nq + i, OFF_QB),
            pl.BlockSpec((n_ctx, HEAD_DIM), lambda b, k, i: (b, OFF_KB // HEAD_DIM + k)),
            pl.BlockSpec((n_ctx, HEAD_DIM), lambda b, k, i: (b, OFF_VB // HEAD_DIM + k)),
            pl.BlockSpec((seq, HEAD_DIM), lambda b, k, i: (b, OFF_KB // HEAD_DIM + k)),
            pl.BlockSpec((seq, HEAD_DIM), lambda b, k, i: (b, OFF_VB // HEAD_DIM + k)),
        ],
        out_specs=pl.BlockSpec((tq, qw), lambda b, k, i: (b * nq + i, k)),
        out_shape=jax.ShapeDtypeStruct((batch * seq, B_WIDTH), BF16),
        compiler_params=_cparams(("arbitrary", "arbitrary", "arbitrary")),
        name="attn_window",
    )(sink, proj, proj, proj, proj_ctx, proj_ctx, proj, proj)


def _attn_ctx_kernel(sink_ref, q0_ref, q1_ref, q2_ref, k_ref, v_ref, o_ref):
    tq = q0_ref.shape[0]
    kv = pl.program_id(1)
    qs = _stack_heads((q0_ref, q1_ref, q2_ref))
    s = lax.dot_general(qs, k_ref[...], _NT, preferred_element_type=F32)
    ss = _sink_column(sink_ref, kv, tq)
    m = jnp.maximum(jnp.max(s, -1, keepdims=True), ss)
    p = jnp.exp(s - m)
    denom = jnp.sum(p, -1, keepdims=True) + jnp.exp(ss - m)
    o = jnp.dot(p.astype(BF16), v_ref[...], preferred_element_type=F32)
    o_ref[...] = _unstack_heads(o / denom, tq).astype(o_ref.dtype)


def _context_attention(sink, proj_ctx, batch, n_ctx, off_q, off_k, off_v, name):
    qw = GROUP * HEAD_DIM
    n_kv = A_KV
    return pl.pallas_call(
        _attn_ctx_kernel,
        grid=(batch, n_kv),
        in_specs=[
            pl.BlockSpec(memory_space=pltpu.SMEM),
            *_q_specs(n_ctx, lambda b, k: b, off_q),
            pl.BlockSpec((n_ctx, HEAD_DIM), lambda b, k: (b, off_k // HEAD_DIM + k)),
            pl.BlockSpec((n_ctx, HEAD_DIM), lambda b, k: (b, off_v // HEAD_DIM + k)),
        ],
        out_specs=pl.BlockSpec((n_ctx, qw), lambda b, k: (b, k)),
        out_shape=jax.ShapeDtypeStruct((batch * n_ctx, n_kv * qw), BF16),
        compiler_params=_cparams(("arbitrary", "arbitrary")),
        name=name,
    )(sink, proj_ctx, proj_ctx, proj_ctx, proj_ctx, proj_ctx)


def _outproj_kernel(h_ref, mod_ref, ya_ref, yb_ref, u_ref, v_ref, ws_ref, bs_ref, gg_ref, w_ref, o_ref, y_ref):
    j = pl.program_id(1)

    @pl.when(j == 0)
    def _():
        tm = ya_ref.shape[0]
        u = u_ref[...]
        v = v_ref[...]
        rows = []
        for c in range(tm // CHUNK):
            r0 = c * CHUNK
            cols = []
            for g in range(C_GROUPS):
                c0 = g * C_GROUP_DIM
                s = jnp.dot(ws_ref[g], v[r0:r0 + CHUNK, c0:c0 + C_GROUP_DIM],
                            preferred_element_type=F32) + bs_ref[g]
                cols.append(u[r0:r0 + CHUNK, c0:c0 + C_GROUP_DIM].astype(F32) * s)
            rows.append(jnp.concatenate(cols, 1))
        yc = jnp.concatenate(rows, 0)
        ya = ya_ref[...].astype(F32)
        yb = yb_ref[...].astype(F32)
        y = jnp.concatenate([ya * _rms_scale(ya), yb * _rms_scale(yb), yc * _rms_scale(yc)], 1) * gg_ref[...]
        y_ref[...] = y.astype(BF16)

    acc = jnp.dot(y_ref[...], w_ref[...], preferred_element_type=F32)
    o_ref[...] = h_ref[...] + mod_ref[0, 2:3, :] * acc


def _out_projection(h, mod, mod_row_map, ya, yb, proj, ws, bs_b, g_group, w_out, name):
    m, d = h.shape
    tm = _tile(m, TOKEN_TILE)
    tn = _tile(d, OUT_COL_TILE)
    return pl.pallas_call(
        _outproj_kernel,
        grid=(m // tm, d // tn),
        in_specs=[
            pl.BlockSpec((tm, tn), lambda i, j: (i, j)),
            pl.BlockSpec((1, N_MOD, tn), lambda i, j: (mod_row_map(i), 0, j)),
            pl.BlockSpec((tm, A_WIDTH), lambda i, j: (i, 0)),
            pl.BlockSpec((tm, B_WIDTH), lambda i, j: (i, 0)),
            pl.BlockSpec((tm, C_WIDTH), lambda i, j: (i, OFF_UC // C_WIDTH)),
            pl.BlockSpec((tm, C_WIDTH), lambda i, j: (i, OFF_VC // C_WIDTH)),
            pl.BlockSpec((C_GROUPS, CHUNK, CHUNK), lambda i, j: (0, 0, 0)),
            pl.BlockSpec((C_GROUPS, CHUNK, C_GROUP_DIM), lambda i, j: (0, 0, 0)),
            pl.BlockSpec((1, MIX_WIDTH), lambda i, j: (0, 0)),
            pl.BlockSpec((MIX_WIDTH, tn), lambda i, j: (0, j)),
        ],
        out_specs=pl.BlockSpec((tm, tn), lambda i, j: (i, j)),
        out_shape=jax.ShapeDtypeStruct((m, d), F32),
        scratch_shapes=[pltpu.VMEM((tm, MIX_WIDTH), BF16)],
        compiler_params=_cparams(("arbitrary", "arbitrary")),
        name=name,
    )(h, mod, ya, yb, proj, proj, ws, bs_b, g_group, w_out)


def _top16(s):
    nk, n = s.shape
    idx = lax.broadcasted_iota(jnp.int32, (nk, n), 0)
    row16 = lax.broadcasted_iota(jnp.int32, (PEER_TOPK, n), 0)
    vals = jnp.zeros((PEER_TOPK, n), F32)
    rank = jnp.full((nk, n), float(PEER_TOPK), F32)
    work = s
    for r in range(PEER_TOPK):
        mx = jnp.max(work, 0, keepdims=True)
        first = jnp.min(jnp.where(work == mx, idx, nk), 0, keepdims=True)
        hit = idx == first
        rank = jnp.where(hit, float(r), rank)
        vals = jnp.where(row16 == r, mx, vals)
        work = jnp.where(hit, -jnp.inf, work)
    return vals, rank


def _staircase(v1, v2):
    n = v1.shape[1]
    k = PEER_TOPK
    half = k // 2
    slabs, cis = [], []
    r16 = lax.broadcasted_iota(jnp.int32, (k, n), 0)
    r8 = lax.broadcasted_iota(jnp.int32, (half, n), 0)
    slabs.append(v1 + v2[0:1])
    cis.append(r16 * k)
    for b in range(1, half):
        slabs.append(v1[0:half] + v2[b:b + 1])
        cis.append(r8 * k + b)
    slabs.append(v1[0:1] + v2[half:k])
    cis.append(r8 + half)
    cand = jnp.concatenate(slabs, 0)
    ci = jnp.concatenate(cis, 0)
    big = k * k
    work = cand
    sel = jnp.zeros(cand.shape, jnp.bool_)
    for _ in range(k):
        mx = jnp.max(work, 0, keepdims=True)
        first = jnp.min(jnp.where(work == mx, ci, big), 0, keepdims=True)
        hit = ci == first
        sel = jnp.logical_or(sel, hit)
        work = jnp.where(hit, -jnp.inf, work)
    top = cand[0:1]
    z = jnp.sum(jnp.where(sel, jnp.exp(cand - top), 0.0), 0, keepdims=True)
    self = sel.astype(F32)
    lo = self[0:half]
    for b in range(1, half):
        lo = lo + self[k + (b - 1) * half:k + b * half]
    tail = jnp.sum(self[k + (half - 1) * half:], 0, keepdims=True)
    lo = lo + jnp.where(r8 == 0, tail, 0.0)
    counts = jnp.concatenate([lo, self[half:k]], 0)
    return counts, z


def _peer_route_kernel(h_ref, mod_ref, g_ref, wqt_ref, keys_ref, hnt_ref, lr_ref, e1_ref, rb_ref, e2_ref, qt_ref):
    hd = pl.program_id(1)
    tm = h_ref.shape[0]

    @pl.when(hd == 0)
    def _():
        x = h_ref[...]
        hn = x * _rms_scale(x) * g_ref[...]
        hn = hn * (1.0 + mod_ref[0, 4:5, :]) + mod_ref[0, 3:4, :]
        hnt = hn.T.astype(BF16)
        hnt_ref[...] = hnt
        qt_ref[...] = jnp.dot(wqt_ref[...], hnt, preferred_element_type=F32).astype(BF16)

    q0 = pl.multiple_of(hd * PEER_QDIM, PEER_QDIM)
    k1 = keys_ref[2 * hd]
    k2 = keys_ref[2 * hd + 1]

    def chunk(c, carry):
        t0 = pl.multiple_of(c * LANE, LANE)
        s1 = jnp.dot(k1, qt_ref[pl.ds(q0, PEER_HALF), pl.ds(t0, LANE)], preferred_element_type=F32)
        s2 = jnp.dot(k2, qt_ref[pl.ds(q0 + PEER_HALF, PEER_HALF), pl.ds(t0, LANE)], preferred_element_type=F32)
        v1, rank1 = _top16(s1)
        v2, rank2 = _top16(s2)
        counts, z = _staircase(v1, v2)
        lr = jnp.zeros_like(rank1)
        for a in range(PEER_TOPK):
            lr = jnp.where(rank1 == float(a), counts[a:a + 1], lr)
        lr_ref[0, :, pl.ds(t0, LANE)] = lr
        e1_ref[0, :, pl.ds(t0, LANE)] = jnp.exp(s1 - v1[0:1]) / z
        rb_ref[0, :, pl.ds(t0, LANE)] = rank2
        e2_ref[0, :, pl.ds(t0, LANE)] = jnp.exp(s2 - v2[0:1])
        return carry

    lax.fori_loop(0, tm // LANE, chunk, 0)


def _peer_route(h, mod, mod_row_map, norm_g, wq_t, keys, name):
    m, d = h.shape
    tm = _tile(m, TOKEN_TILE)
    qd = wq_t.shape[0]
    gate_spec = pl.BlockSpec((1, PEER_KEYS, tm), lambda i, hd: (hd, 0, i))
    gate_shape = jax.ShapeDtypeStruct((PEER_HEADS, PEER_KEYS, m), F32)
    return pl.pallas_call(
        _peer_route_kernel,
        grid=(m // tm, PEER_HEADS),
        in_specs=[
            pl.BlockSpec((tm, d), lambda i, hd: (i, 0)),
            pl.BlockSpec((1, N_MOD, d), lambda i, hd: (mod_row_map(i), 0, 0)),
            pl.BlockSpec((1, d), lambda i, hd: (0, 0)),
            pl.BlockSpec((qd, d), lambda i, hd: (0, 0)),
            pl.BlockSpec((2 * PEER_HEADS, PEER_KEYS, PEER_HALF), lambda i, hd: (0, 0, 0)),
        ],
        out_specs=[pl.BlockSpec((d, tm), lambda i, hd: (0, i)), gate_spec, gate_spec, gate_spec, gate_spec],
        out_shape=[jax.ShapeDtypeStruct((d, m), BF16), gate_shape, gate_shape, gate_shape, gate_shape],
        scratch_shapes=[pltpu.VMEM((qd, tm), BF16)],
        compiler_params=_cparams(("arbitrary", "arbitrary")),
        name=name,
    )(h, mod, norm_g, wq_t, keys)


def _peer_dense_kernel(h_ref, mod_ref, hnt_ref, lr_ref, e1_ref, rb_ref, e2_ref, u_ref, vt_ref, o_ref,
                       acc_ref, hid_ref, a_ref):
    j = pl.program_id(1)
    te, tm = hid_ref.shape
    per_step = te // PEER_KEYS

    @pl.when(j == 0)
    def _():
        acc_ref[...] = jnp.zeros_like(acc_ref)

    hid_ref[...] = jnp.dot(u_ref[...], hnt_ref[...], preferred_element_type=F32)
    i1_base = pl.multiple_of(j * per_step, SUBLANE)
    for c in range(tm // LANE):
        t0 = c * LANE
        lr_rows = [lr_ref[hd, pl.ds(i1_base, per_step), t0:t0 + LANE] for hd in range(PEER_HEADS)]
        e1_rows = [e1_ref[hd, pl.ds(i1_base, per_step), t0:t0 + LANE] for hd in range(PEER_HEADS)]
        for il in range(per_step):
            r0 = il * PEER_KEYS
            gate = jnp.zeros((PEER_KEYS, LANE), F32)
            for hd in range(PEER_HEADS):
                sel = rb_ref[hd, :, t0:t0 + LANE] < lr_rows[hd][il:il + 1]
                gate = gate + jnp.where(sel, e2_ref[hd, :, t0:t0 + LANE], 0.0) * e1_rows[hd][il:il + 1]
            act = jax.nn.gelu(hid_ref[r0:r0 + PEER_KEYS, t0:t0 + LANE])
            a_ref[r0:r0 + PEER_KEYS, t0:t0 + LANE] = (act * gate).astype(BF16)
    acc_ref[...] += jnp.dot(vt_ref[...], a_ref[...], preferred_element_type=F32)

    @pl.when(j == pl.num_programs(1) - 1)
    def _():
        o_ref[...] = h_ref[...] + mod_ref[0, 5:6, :] * acc_ref[...].T


def _peer_dense(h, mod, mod_row_map, hnt, lr, e1, rb, e2, u, vt, name):
    m, d = h.shape
    n_exp = u.shape[0]
    tm = _tile(m, TOKEN_TILE)
    te = _tile(n_exp, EXPERT_TILE)
    once = dict(pipeline_mode=pl.Buffered(1))
    gate_spec = pl.BlockSpec((PEER_HEADS, PEER_KEYS, tm), lambda i, j: (0, 0, i), **once)
    return pl.pallas_call(
        _peer_dense_kernel,
        grid=(m // tm, n_exp // te),
        in_specs=[
            pl.BlockSpec((tm, d), lambda i, j: (i, 0), **once),
            pl.BlockSpec((1, N_MOD, d), lambda i, j: (mod_row_map(i), 0, 0)),
            pl.BlockSpec((d, tm), lambda i, j: (0, i), **once),
            gate_spec, gate_spec, gate_spec, gate_spec,
            pl.BlockSpec((te, d), lambda i, j: (j, 0)),
            pl.BlockSpec((d, te), lambda i, j: (0, j)),
        ],
        out_specs=pl.BlockSpec((tm, d), lambda i, j: (i, 0)),
        out_shape=jax.ShapeDtypeStruct((m, d), F32),
        scratch_shapes=[pltpu.VMEM((d, tm), F32), pltpu.VMEM((te, tm), F32), pltpu.VMEM((te, tm), BF16)],
        compiler_params=_cparams(("arbitrary", "arbitrary")),
        name=name,
    )(h, mod, hnt, lr, e1, rb, e2, u, vt)


def _rope_tables(seq):
    t = jnp.arange(seq)
    row = (t // GRID_W).astype(F32)
    col = (t % GRID_W).astype(F32)
    inv = ROPE_THETA ** (-jnp.arange(AXIS_DIM // 2, dtype=F32) / (AXIS_DIM // 2))
    ar, ac = row[:, None] * inv, col[:, None] * inv
    ang = jnp.concatenate([ar, ar, ac, ac], -1)
    sign = jnp.tile(jnp.concatenate([-jnp.ones(AXIS_DIM // 2, F32), jnp.ones(AXIS_DIM // 2, F32)]), 2)
    return jnp.cos(ang), jnp.sin(ang) * sign


def _gain_table(qn_a, kn_a, qn_b, kn_b, vn_c):
    scale = HEAD_DIM ** -0.5
    ones_a = jnp.ones((A_KV_WIDTH,), F32)
    ones_b = jnp.ones((B_KV_WIDTH,), F32)
    return jnp.concatenate([
        jnp.tile(qn_a * scale, A_HEADS), jnp.tile(kn_a, A_KV), ones_a,
        jnp.tile(qn_b * scale, B_HEADS), jnp.tile(kn_b, B_KV), ones_b,
        jnp.ones((C_WIDTH,), F32), vn_c,
    ]).reshape(1, IN_WIDTH)


def kernel(x, c, ctx, c_ctx, w_ada, b_ada, norm1_g, norm2_g, w_in, qn_a, kn_a, qn_b, kn_b, sink_b, vn_c, ws_c,
           bs_c, g_group, w_out, peer_wq, peer_keys, peer_u, peer_v):
    batch, seq, d = x.shape
    n_ctx = ctx.shape[1]
    depth = w_ada.shape[0]
    assert batch < MOD_ROWS and seq % TOKEN_TILE == 0 and TOKEN_TILE % n_ctx == 0
    assert seq % GRID_W == 0 and n_ctx % CHUNK == 0

    ctx_row = batch
    cc = jnp.zeros((MOD_ROWS, d), F32).at[:batch].set(c).at[ctx_row].set(c_ctx)
    mod_all = _ada_modulation(cc, w_ada, b_ada).reshape(depth, MOD_ROWS, N_MOD, d)

    cos, sin_signed = _rope_tables(seq)
    tiles_per_seq = seq // TOKEN_TILE
    lat_row = lambda i: i // tiles_per_seq
    ctx_row_map = lambda i: ctx_row
    no_sink = jnp.full((A_HEADS,), NEG_INF, F32)

    h = x.reshape(batch * seq, d)
    hc = ctx.reshape(batch * n_ctx, d)
    for l in range(depth):
        last = l == depth - 1
        mod = mod_all[l]
        w_in_l = w_in[l].astype(BF16)
        w_out_l = w_out[l].astype(BF16)
        gains = _gain_table(qn_a[l], kn_a[l], qn_b[l], kn_b[l], vn_c[l])
        n1 = norm1_g[l].reshape(1, d)
        n2 = norm2_g[l].reshape(1, d)
        ws = ws_c[l].astype(BF16)
        bs_b = jnp.broadcast_to(bs_c[l][:, :, None], (C_GROUPS, CHUNK, C_GROUP_DIM))
        gg = g_group[l].reshape(1, MIX_WIDTH)
        wq_t = peer_wq[l].T.astype(BF16)
        keys = peer_keys[l].reshape(2 * PEER_HEADS, PEER_KEYS, PEER_HALF).astype(BF16)
        u_tab = peer_u[l].astype(BF16)
        v_t = peer_v[l].T.astype(BF16)

        proj = _in_projection(h, mod, lat_row, n1, w_in_l, gains, cos, sin_signed,
                              tiles_per_seq=tiles_per_seq, use_rope=True)
        proj_c = _in_projection(hc, mod, ctx_row_map, n1, w_in_l, gains, cos, sin_signed,
                                tiles_per_seq=tiles_per_seq, use_rope=False)
        ya = _global_attention(proj, proj_c, batch, seq, n_ctx)
        yb = _window_attention(sink_b[l], proj, proj_c, batch, seq, n_ctx)
        h = _out_projection(h, mod, lat_row, ya, yb, proj, ws, bs_b, gg, w_out_l, "out_projection")
        routed = _peer_route(h, mod, lat_row, n2, wq_t, keys, "peer_route")
        h = _peer_dense(h, mod, lat_row, *routed, u_tab, v_t, "peer_dense")
        if not last:
            cya = _context_attention(no_sink, proj_c, batch, n_ctx, OFF_QA, OFF_KA, OFF_VA, "attn_ctx_a")
            cyb = _context_attention(sink_b[l], proj_c, batch, n_ctx, OFF_QB, OFF_KB, OFF_VB, "attn_ctx_b")
            hc = _out_projection(hc, mod, ctx_row_map, cya, cyb, proj_c, ws, bs_b, gg, w_out_l, "out_projection_ctx")
            routed_c = _peer_route(hc, mod, ctx_row_map, n2, wq_t, keys, "peer_route_ctx")
            hc = _peer_dense(hc, mod, ctx_row_map, *routed_c, u_tab, v_t, "peer_dense_ctx")
    return h.reshape(batch, seq, d)
```

```python
import functools

import jax
import jax.numpy as jnp
from jax import lax
from jax.experimental import pallas as pl
from jax.experimental.pallas import tpu as pltpu

F32 = jnp.float32
BF16 = jnp.bfloat16

GRID_W = 64
HEAD_DIM = 128
AXIS_DIM = HEAD_DIM // 2
ROPE_THETA = 10000.0
A_HEADS, A_KV = 6, 2
B_HEADS, B_KV = 6, 2
WINDOW = 128
C_GROUPS, C_GROUP_DIM, CHUNK = 4, 128, 128
A_WIDTH = A_HEADS * HEAD_DIM
B_WIDTH = B_HEADS * HEAD_DIM
C_WIDTH = C_GROUPS * C_GROUP_DIM
MIX_WIDTH = A_WIDTH + B_WIDTH + C_WIDTH
A_KV_WIDTH = A_KV * HEAD_DIM
B_KV_WIDTH = B_KV * HEAD_DIM
IN_WIDTH = A_WIDTH + 2 * A_KV_WIDTH + B_WIDTH + 2 * B_KV_WIDTH + 2 * C_WIDTH
GROUP = A_HEADS // A_KV
PEER_HEADS = 8
PEER_KEYS = 128
PEER_QDIM = 256
PEER_HALF = PEER_QDIM // 2
PEER_TOPK = 16
N_MOD = 6
EPS = 1e-6
NEG_INF = -1e30

OFF_QA = 0
OFF_KA = OFF_QA + A_WIDTH
OFF_VA = OFF_KA + A_KV_WIDTH
OFF_QB = OFF_VA + A_KV_WIDTH
OFF_KB = OFF_QB + B_WIDTH
OFF_VB = OFF_KB + B_KV_WIDTH
OFF_UC = OFF_VB + B_KV_WIDTH
OFF_VC = OFF_UC + C_WIDTH

MOD_ROWS = 16
TOKEN_TILE = 512
IN_COL_TILE = 256
OUT_COL_TILE = 512
ADA_COL_TILE = 1024
Q_TILE_GLOBAL = 256
Q_TILE_WINDOW = 128
EXPERT_TILE = 1024
LANE = 128
SUBLANE = 8
VMEM_LIMIT = 56 * 1024 * 1024


def _tile(n, preferred):
    t = min(preferred, n)
    while n % t:
        t -= LANE
    return t


def _cparams(sem):
    return pltpu.CompilerParams(dimension_semantics=sem, vmem_limit_bytes=VMEM_LIMIT)


def _rms_scale(x):
    return lax.rsqrt(jnp.mean(x * x, -1, keepdims=True) + EPS)


def _ada_kernel(c_ref, w_ref, b_ref, o_ref):
    c = c_ref[...]
    a = (c * jax.nn.sigmoid(c)).astype(BF16)
    o_ref[0] = jnp.dot(a, w_ref[0].astype(BF16), preferred_element_type=F32) + b_ref[0]


def _ada_modulation(cc, w_ada, b_ada):
    depth, d, n = w_ada.shape
    tn = _tile(n, ADA_COL_TILE)
    return pl.pallas_call(
        _ada_kernel,
        grid=(depth, n // tn),
        in_specs=[
            pl.BlockSpec((MOD_ROWS, d), lambda l, j: (0, 0)),
            pl.BlockSpec((1, d, tn), lambda l, j: (l, 0, j)),
            pl.BlockSpec((1, 1, tn), lambda l, j: (l, 0, j)),
        ],
        out_specs=pl.BlockSpec((1, MOD_ROWS, tn), lambda l, j: (l, 0, j)),
        out_shape=jax.ShapeDtypeStruct((depth, MOD_ROWS, n), F32),
        compiler_params=_cparams(("arbitrary", "arbitrary")),
        name="ada_modulation",
    )(cc, w_ada, b_ada.reshape(depth, 1, n))


def _col_tile_kinds():
    kinds = []
    for j in range(IN_WIDTH // IN_COL_TILE):
        c = j * IN_COL_TILE
        if c < OFF_VA or OFF_QB <= c < OFF_VB:
            kinds.append("head")
        elif c < OFF_QB or c < OFF_UC:
            kinds.append("plain")
        elif c < OFF_VC:
            kinds.append("gelu")
        else:
            kinds.append("gelu_norm")
    return kinds


def _inproj_kernel(h_ref, mod_ref, g_ref, w_ref, gain_ref, cos_ref, sin_ref, o_ref, *, use_rope):
    x = h_ref[...]
    xn = x * _rms_scale(x) * g_ref[...]
    xn = (xn * (1.0 + mod_ref[0, 1:2, :]) + mod_ref[0, 0:1, :]).astype(BF16)

    def rope(a):
        if not use_rope:
            return a
        lane = lax.broadcasted_iota(jnp.int32, a.shape, 1)
        first = (lane // (AXIS_DIM // 2)) % 2 == 0
        rot = jnp.where(first, pltpu.roll(a, HEAD_DIM - AXIS_DIM // 2, 1), pltpu.roll(a, AXIS_DIM // 2, 1))
        return a * cos_ref[...] + rot * sin_ref[...]

    def gelu_norm(a, g):
        a = jax.nn.gelu(a)
        return a * _rms_scale(a) * g

    epilogue = {
        "head": lambda a, g: rope(a * _rms_scale(a) * g),
        "plain": lambda a, g: a,
        "gelu": lambda a, g: jax.nn.gelu(a),
        "gelu_norm": gelu_norm,
    }
    for j, kind in enumerate(_col_tile_kinds()):
        c0 = j * IN_COL_TILE
        acc = jnp.dot(xn, w_ref[:, c0:c0 + IN_COL_TILE], preferred_element_type=F32)
        for k in range(IN_COL_TILE // LANE):
            lo = c0 + k * LANE
            part = epilogue[kind](acc[:, k * LANE:(k + 1) * LANE], gain_ref[:, lo:lo + LANE])
            o_ref[:, lo:lo + LANE] = part.astype(o_ref.dtype)


def _in_projection(h, mod, mod_row_map, norm_g, w_in, gains, cos, sin_signed, *, tiles_per_seq, use_rope):
    m, d = h.shape
    tm = _tile(m, TOKEN_TILE)
    if use_rope:
        pos_map = lambda i: (i % tiles_per_seq, 0)
    else:
        pos_map = lambda i: (0, 0)
    once = dict(pipeline_mode=pl.Buffered(1))
    return pl.pallas_call(
        functools.partial(_inproj_kernel, use_rope=use_rope),
        grid=(m // tm,),
        in_specs=[
            pl.BlockSpec((tm, d), lambda i: (i, 0)),
            pl.BlockSpec((1, N_MOD, d), lambda i: (mod_row_map(i), 0, 0)),
            pl.BlockSpec((1, d), lambda i: (0, 0)),
            pl.BlockSpec((d, IN_WIDTH), lambda i: (0, 0), **once),
            pl.BlockSpec((1, IN_WIDTH), lambda i: (0, 0)),
            pl.BlockSpec((tm, HEAD_DIM), pos_map),
            pl.BlockSpec((tm, HEAD_DIM), pos_map),
        ],
        out_specs=pl.BlockSpec((tm, IN_WIDTH), lambda i: (i, 0)),
        out_shape=jax.ShapeDtypeStruct((m, IN_WIDTH), BF16),
        compiler_params=_cparams(("arbitrary",)),
        name="in_projection_rope" if use_rope else "in_projection_ctx",
    )(h, mod, norm_g, w_in, gains, cos, sin_signed)


_NT = (((1,), (1,)), ((), ()))


def _stack_heads(q_refs):
    return jnp.concatenate([r[...] for r in q_refs], 0)


def _q_specs(rows, row_fn, off_q):
    def spec(g):
        return pl.BlockSpec((rows, HEAD_DIM),
                            lambda *ids: (row_fn(*ids), off_q // HEAD_DIM + ids[1] * GROUP + g))
    return [spec(g) for g in range(GROUP)]


def _unstack_heads(o, tq):
    return jnp.concatenate([o[g * tq:(g + 1) * tq] for g in range(GROUP)], 1)


def _sink_column(sink_ref, kv, tq):
    row = lax.broadcasted_iota(jnp.int32, (GROUP * tq, 1), 0)
    col = jnp.full((GROUP * tq, 1), sink_ref[kv * GROUP + GROUP - 1], F32)
    for g in range(GROUP - 2, -1, -1):
        col = jnp.where(row < (g + 1) * tq, sink_ref[kv * GROUP + g], col)
    return col


def _attn_global_kernel(q0_ref, q1_ref, q2_ref, kc_ref, vc_ref, kl_ref, vl_ref, o_ref):
    tq = q0_ref.shape[0]
    qs = _stack_heads((q0_ref, q1_ref, q2_ref))
    sc = lax.dot_general(qs, kc_ref[...], _NT, preferred_element_type=F32)
    sl = lax.dot_general(qs, kl_ref[...], _NT, preferred_element_type=F32)
    m = jnp.maximum(jnp.max(sc, -1, keepdims=True), jnp.max(sl, -1, keepdims=True))
    pc = jnp.exp(sc - m)
    pw = jnp.exp(sl - m)
    denom = jnp.sum(pc, -1, keepdims=True) + jnp.sum(pw, -1, keepdims=True)
    o = (jnp.dot(pc.astype(BF16), vc_ref[...], preferred_element_type=F32)
         + jnp.dot(pw.astype(BF16), vl_ref[...], preferred_element_type=F32))
    o_ref[...] = _unstack_heads(o / denom, tq).astype(o_ref.dtype)


def _global_attention(proj, proj_ctx, batch, seq, n_ctx):
    tq = _tile(seq, Q_TILE_GLOBAL)
    nq = seq // tq
    qw = GROUP * HEAD_DIM
    return pl.pallas_call(
        _attn_global_kernel,
        grid=(batch, A_KV, nq),
        in_specs=[
            *_q_specs(tq, lambda b, k, i: b * nq + i, OFF_QA),
            pl.BlockSpec((n_ctx, HEAD_DIM), lambda b, k, i: (b, OFF_KA // HEAD_DIM + k)),
            pl.BlockSpec((n_ctx, HEAD_DIM), lambda b, k, i: (b, OFF_VA // HEAD_DIM + k)),
            pl.BlockSpec((seq, HEAD_DIM), lambda b, k, i: (b, OFF_KA // HEAD_DIM + k)),
            pl.BlockSpec((seq, HEAD_DIM), lambda b, k, i: (b, OFF_VA // HEAD_DIM + k)),
        ],
        out_specs=pl.BlockSpec((tq, qw), lambda b, k, i: (b * nq + i, k)),
        out_shape=jax.ShapeDtypeStruct((batch * seq, A_WIDTH), BF16),
        compiler_params=_cparams(("arbitrary", "arbitrary", "arbitrary")),
        name="attn_global",
    )(proj, proj, proj, proj_ctx, proj_ctx, proj, proj)


def _attn_window_kernel(sink_ref, q0_ref, q1_ref, q2_ref, kc_ref, vc_ref, kl_ref, vl_ref, o_ref):
    tq = q0_ref.shape[0]
    seq = kl_ref.shape[0]
    span = tq + 2 * WINDOW
    kv = pl.program_id(1)
    i = pl.program_id(2)
    start = pl.multiple_of(jnp.clip(i * tq - WINDOW, 0, seq - span), LANE)
    qs = _stack_heads((q0_ref, q1_ref, q2_ref))
    sw = lax.dot_general(qs, kl_ref[pl.ds(start, span), :], _NT, preferred_element_type=F32)
    qpos = i * tq + lax.broadcasted_iota(jnp.int32, (tq, span), 0)
    kpos = start + lax.broadcasted_iota(jnp.int32, (tq, span), 1)
    valid = jnp.abs(kpos - qpos) <= WINDOW
    sw = jnp.where(jnp.concatenate([valid] * GROUP, 0), sw, NEG_INF)
    sc = lax.dot_general(qs, kc_ref[...], _NT, preferred_element_type=F32)
    ss = _sink_column(sink_ref, kv, tq)
    m = jnp.maximum(jnp.maximum(jnp.max(sw, -1, keepdims=True), jnp.max(sc, -1, keepdims=True)), ss)
    pw = jnp.exp(sw - m)
    pc = jnp.exp(sc - m)
    denom = jnp.sum(pw, -1, keepdims=True) + jnp.sum(pc, -1, keepdims=True) + jnp.exp(ss - m)
    o = (jnp.dot(pw.astype(BF16), vl_ref[pl.ds(start, span), :], preferred_element_type=F32)
         + jnp.dot(pc.astype(BF16), vc_ref[...], preferred_element_type=F32))
    o_ref[...] = _unstack_heads(o / denom, tq).astype(o_ref.dtype)


def _window_attention(sink, proj, proj_ctx, batch, seq, n_ctx):
    tq = Q_TILE_WINDOW
    nq = seq // tq
    qw = GROUP * HEAD_DIM
    return pl.pallas_call(
        _attn_window_kernel,
        grid=(batch, B_KV, nq),
        in_specs=[
            pl.BlockSpec(memory_space=pltpu.SMEM),
            *_q_specs(tq, lambda b, k, i: b * nq + i, OFF_QB),
            pl.BlockSpec((n_ctx, HEAD_DIM), lambda b, k, i: (b, OFF_KB // HEAD_DIM + k)),
            pl.BlockSpec((n_ctx, HEAD_DIM), lambda b, k, i: (b, OFF_VB // HEAD_DIM + k)),
            pl.BlockSpec((seq, HEAD_DIM), lambda b, k, i: (b, OFF_KB // HEAD_DIM + k)),
            pl.BlockSpec((seq, HEAD_DIM), lambda b, k, i: (b, OFF_VB // HEAD_DIM + k)),
        ],
        out_specs=pl.BlockSpec((tq, qw), lambda b, k, i: (b * nq + i, k)),
        out_shape=jax.ShapeDtypeStruct((batch * seq, B_WIDTH), BF16),
        compiler_params=_cparams(("arbitrary", "arbitrary", "arbitrary")),
        name="attn_window",
    )(sink, proj, proj, proj, proj_ctx, proj_ctx, proj, proj)


def _attn_ctx_kernel(sink_ref, q0_ref, q1_ref, q2_ref, k_ref, v_ref, o_ref):
    tq = q0_ref.shape[0]
    kv = pl.program_id(1)
    qs = _stack_heads((q0_ref, q1_ref, q2_ref))
    s = lax.dot_general(qs, k_ref[...], _NT, preferred_element_type=F32)
    ss = _sink_column(sink_ref, kv, tq)
    m = jnp.maximum(jnp.max(s, -1, keepdims=True), ss)
    p = jnp.exp(s - m)
    denom = jnp.sum(p, -1, keepdims=True) + jnp.exp(ss - m)
    o = jnp.dot(p.astype(BF16), v_ref[...], preferred_element_type=F32)
    o_ref[...] = _unstack_heads(o / denom, tq).astype(o_ref.dtype)


def _context_attention(sink, proj_ctx, batch, n_ctx, off_q, off_k, off_v, name):
    qw = GROUP * HEAD_DIM
    n_kv = A_KV
    return pl.pallas_call(
        _attn_ctx_kernel,
        grid=(batch, n_kv),
        in_specs=[
            pl.BlockSpec(memory_space=pltpu.SMEM),
            *_q_specs(n_ctx, lambda b, k: b, off_q),
            pl.BlockSpec((n_ctx, HEAD_DIM), lambda b, k: (b, off_k // HEAD_DIM + k)),
            pl.BlockSpec((n_ctx, HEAD_DIM), lambda b, k: (b, off_v // HEAD_DIM + k)),
        ],
        out_specs=pl.BlockSpec((n_ctx, qw), lambda b, k: (b, k)),
        out_shape=jax.ShapeDtypeStruct((batch * n_ctx, n_kv * qw), BF16),
        compiler_params=_cparams(("arbitrary", "arbitrary")),
        name=name,
    )(sink, proj_ctx, proj_ctx, proj_ctx, proj_ctx, proj_ctx)


def _outproj_kernel(h_ref, mod_ref, ya_ref, yb_ref, u_ref, v_ref, ws_ref, bs_ref, gg_ref, w_ref, o_ref, y_ref):
    j = pl.program_id(1)

    @pl.when(j == 0)
    def _():
        tm = ya_ref.shape[0]
        u = u_ref[...]
        v = v_ref[...]
        rows = []
        for c in range(tm // CHUNK):
            r0 = c * CHUNK
            cols = []
            for g in range(C_GROUPS):
                c0 = g * C_GROUP_DIM
                s = jnp.dot(ws_ref[g], v[r0:r0 + CHUNK, c0:c0 + C_GROUP_DIM],
                            preferred_element_type=F32) + bs_ref[g]
                cols.append(u[r0:r0 + CHUNK, c0:c0 + C_GROUP_DIM].astype(F32) * s)
            rows.append(jnp.concatenate(cols, 1))
        yc = jnp.concatenate(rows, 0)
        ya = ya_ref[...].astype(F32)
        yb = yb_ref[...].astype(F32)
        y = jnp.concatenate([ya * _rms_scale(ya), yb * _rms_scale(yb), yc * _rms_scale(yc)], 1) * gg_ref[...]
        y_ref[...] = y.astype(BF16)

    acc = jnp.dot(y_ref[...], w_ref[...], preferred_element_type=F32)
    o_ref[...] = h_ref[...] + mod_ref[0, 2:3, :] * acc


def _out_projection(h, mod, mod_row_map, ya, yb, proj, ws, bs_b, g_group, w_out, name):
    m, d = h.shape
    tm = _tile(m, TOKEN_TILE)
    tn = _tile(d, OUT_COL_TILE)
    return pl.pallas_call(
        _outproj_kernel,
        grid=(m // tm, d // tn),
        in_specs=[
            pl.BlockSpec((tm, tn), lambda i, j: (i, j)),
            pl.BlockSpec((1, N_MOD, tn), lambda i, j: (mod_row_map(i), 0, j)),
            pl.BlockSpec((tm, A_WIDTH), lambda i, j: (i, 0)),
            pl.BlockSpec((tm, B_WIDTH), lambda i, j: (i, 0)),
            pl.BlockSpec((tm, C_WIDTH), lambda i, j: (i, OFF_UC // C_WIDTH)),
            pl.BlockSpec((tm, C_WIDTH), lambda i, j: (i, OFF_VC // C_WIDTH)),
            pl.BlockSpec((C_GROUPS, CHUNK, CHUNK), lambda i, j: (0, 0, 0)),
            pl.BlockSpec((C_GROUPS, CHUNK, C_GROUP_DIM), lambda i, j: (0, 0, 0)),
            pl.BlockSpec((1, MIX_WIDTH), lambda i, j: (0, 0)),
            pl.BlockSpec((MIX_WIDTH, tn), lambda i, j: (0, j)),
        ],
        out_specs=pl.BlockSpec((tm, tn), lambda i, j: (i, j)),
        out_shape=jax.ShapeDtypeStruct((m, d), F32),
        scratch_shapes=[pltpu.VMEM((tm, MIX_WIDTH), BF16)],
        compiler_params=_cparams(("arbitrary", "arbitrary")),
        name=name,
    )(h, mod, ya, yb, proj, proj, ws, bs_b, g_group, w_out)


def _top16(s):
    nk, n = s.shape
    idx = lax.broadcasted_iota(jnp.int32, (nk, n), 0).astype(F32)
    row16 = lax.broadcasted_iota(jnp.int32, (PEER_TOPK, n), 0)
    vals = jnp.zeros((PEER_TOPK, n), F32)
    rank = jnp.full((nk, n), float(PEER_TOPK), F32)
    work = s
    for r in range(PEER_TOPK):
        mx = jnp.max(work, 0, keepdims=True)
        first = jnp.min(jnp.where(work == mx, idx, float(nk)), 0, keepdims=True)
        hit = idx == first
        rank = jnp.where(hit, float(r), rank)
        vals = jnp.where(row16 == r, mx, vals)
        work = jnp.where(hit, -jnp.inf, work)
    return vals, rank


def _staircase(v1, v2):
    n = v1.shape[1]
    k = PEER_TOPK
    half = k // 2
    slabs, cis = [], []
    r16 = lax.broadcasted_iota(jnp.int32, (k, n), 0)
    r8 = lax.broadcasted_iota(jnp.int32, (half, n), 0)
    slabs.append(v1 + v2[0:1])
    cis.append(r16 * k)
    for b in range(1, half):
        slabs.append(v1[0:half] + v2[b:b + 1])
        cis.append(r8 * k + b)
    slabs.append(v1[0:1] + v2[half:k])
    cis.append(r8 + half)
    cand = jnp.concatenate(slabs, 0)
    ci = jnp.concatenate(cis, 0).astype(F32)
    big = float(k * k)
    work = cand
    sel = jnp.zeros(cand.shape, jnp.bool_)
    for _ in range(k):
        mx = jnp.max(work, 0, keepdims=True)
        first = jnp.min(jnp.where(work == mx, ci, big), 0, keepdims=True)
        hit = ci == first
        sel = jnp.logical_or(sel, hit)
        work = jnp.where(hit, -jnp.inf, work)
    top = cand[0:1]
    z = jnp.sum(jnp.where(sel, jnp.exp(cand - top), 0.0), 0, keepdims=True)
    self = sel.astype(F32)
    lo = self[0:half]
    for b in range(1, half):
        lo = lo + self[k + (b - 1) * half:k + b * half]
    tail = jnp.sum(self[k + (half - 1) * half:], 0, keepdims=True)
    lo = lo + jnp.where(r8 == 0, tail, 0.0)
    counts = jnp.concatenate([lo, self[half:k]], 0)
    return counts, z


def _peer_route_kernel(h_ref, mod_ref, g_ref, wqt_ref, keys_ref, hnt_ref, lr_ref, e1_ref, rb_ref, e2_ref, qt_ref):
    hd = pl.program_id(1)
    tm = h_ref.shape[0]

    @pl.when(hd == 0)
    def _():
        x = h_ref[...]
        hn = x * _rms_scale(x) * g_ref[...]
        hn = hn * (1.0 + mod_ref[0, 4:5, :]) + mod_ref[0, 3:4, :]
        hnt = hn.T.astype(BF16)
        hnt_ref[...] = hnt
        qt_ref[...] = jnp.dot(wqt_ref[...], hnt, preferred_element_type=F32).astype(BF16)

    q0 = pl.multiple_of(hd * PEER_QDIM, PEER_QDIM)
    k1 = keys_ref[2 * hd]
    k2 = keys_ref[2 * hd + 1]

    for c in range(tm // LANE):
        t0 = c * LANE
        s1 = jnp.dot(k1, qt_ref[pl.ds(q0, PEER_HALF), t0:t0 + LANE], preferred_element_type=F32)
        s2 = jnp.dot(k2, qt_ref[pl.ds(q0 + PEER_HALF, PEER_HALF), t0:t0 + LANE], preferred_element_type=F32)
        v1, rank1 = _top16(s1)
        v2, rank2 = _top16(s2)
        counts, z = _staircase(v1, v2)
        lr = jnp.zeros_like(rank1)
        for a in range(PEER_TOPK):
            lr = jnp.where(rank1 == float(a), counts[a:a + 1], lr)
        lr_ref[0, :, t0:t0 + LANE] = lr
        e1_ref[0, :, t0:t0 + LANE] = jnp.exp(s1 - v1[0:1]) / z
        rb_ref[0, :, t0:t0 + LANE] = rank2.astype(rb_ref.dtype)
        e2_ref[0, :, t0:t0 + LANE] = jnp.exp(s2 - v2[0:1]).astype(e2_ref.dtype)


def _peer_route(h, mod, mod_row_map, norm_g, wq_t, keys, name):
    m, d = h.shape
    tm = _tile(m, TOKEN_TILE)
    qd = wq_t.shape[0]
    gate_spec = pl.BlockSpec((1, PEER_KEYS, tm), lambda i, hd: (hd, 0, i))
    gate_shape = jax.ShapeDtypeStruct((PEER_HEADS, PEER_KEYS, m), F32)
    gate_shape_bf16 = jax.ShapeDtypeStruct((PEER_HEADS, PEER_KEYS, m), BF16)
    return pl.pallas_call(
        _peer_route_kernel,
        grid=(m // tm, PEER_HEADS),
        in_specs=[
            pl.BlockSpec((tm, d), lambda i, hd: (i, 0)),
            pl.BlockSpec((1, N_MOD, d), lambda i, hd: (mod_row_map(i), 0, 0)),
            pl.BlockSpec((1, d), lambda i, hd: (0, 0)),
            pl.BlockSpec((qd, d), lambda i, hd: (0, 0)),
            pl.BlockSpec((2 * PEER_HEADS, PEER_KEYS, PEER_HALF), lambda i, hd: (0, 0, 0)),
        ],
        out_specs=[pl.BlockSpec((d, tm), lambda i, hd: (0, i)), gate_spec, gate_spec, gate_spec, gate_spec],
        out_shape=[jax.ShapeDtypeStruct((d, m), BF16), gate_shape, gate_shape, gate_shape_bf16, gate_shape_bf16],
        scratch_shapes=[pltpu.VMEM((qd, tm), BF16)],
        compiler_params=_cparams(("arbitrary", "arbitrary")),
        name=name,
    )(h, mod, norm_g, wq_t, keys)


def _peer_dense_kernel(h_ref, mod_ref, hnt_ref, lr_ref, e1_ref, rb_ref, e2_ref, u_ref, vt_ref, o_ref,
                       acc_ref, hid_even, hid_odd, a_ref, *, n_tiles):
    j = pl.program_id(1)
    te, tm = hid_even.shape
    per_step = te // PEER_KEYS

    def project(hid_ref):
        hid_ref[...] = jnp.dot(u_ref[...], hnt_ref[...], preferred_element_type=F32)

    def finish(hid_ref, jj):
        i1_base = pl.multiple_of(jj * per_step, SUBLANE)
        for c in range(tm // LANE):
            t0 = c * LANE
            lr_rows = [lr_ref[hd, pl.ds(i1_base, per_step), t0:t0 + LANE] for hd in range(PEER_HEADS)]
            e1_rows = [e1_ref[hd, pl.ds(i1_base, per_step), t0:t0 + LANE] for hd in range(PEER_HEADS)]
            for il in range(per_step):
                r0 = il * PEER_KEYS
                gate = jnp.zeros((PEER_KEYS, LANE), BF16)
                for hd in range(PEER_HEADS):
                    lr = jnp.broadcast_to(lr_rows[hd][il:il + 1], (PEER_KEYS, LANE)).astype(BF16)
                    e1 = jnp.broadcast_to(e1_rows[hd][il:il + 1], (PEER_KEYS, LANE)).astype(BF16)
                    sel = rb_ref[hd, :, t0:t0 + LANE] < lr
                    gate = gate + jnp.where(sel, e2_ref[hd, :, t0:t0 + LANE], jnp.zeros_like(e1)) * e1
                act = jax.nn.gelu(hid_ref[r0:r0 + PEER_KEYS, t0:t0 + LANE])
                a_ref[r0:r0 + PEER_KEYS, t0:t0 + LANE] = act.astype(BF16) * gate
        acc_ref[...] += jnp.dot(vt_ref[...], a_ref[...], preferred_element_type=F32)

    @pl.when(j == 0)
    def _():
        acc_ref[...] = jnp.zeros_like(acc_ref)
        project(hid_even)

    middle = jnp.logical_and(j > 0, j < n_tiles)

    @pl.when(jnp.logical_and(middle, j % 2 == 1))
    def _():
        project(hid_odd)
        finish(hid_even, j - 1)

    @pl.when(jnp.logical_and(middle, j % 2 == 0))
    def _():
        project(hid_even)
        finish(hid_odd, j - 1)

    @pl.when(j == n_tiles)
    def _():
        finish(hid_odd if n_tiles % 2 == 0 else hid_even, n_tiles - 1)
        o_ref[...] = h_ref[...] + mod_ref[0, 5:6, :] * acc_ref[...].T


def _peer_dense(h, mod, mod_row_map, hnt, lr, e1, rb, e2, u, vt, name):
    m, d = h.shape
    n_exp = u.shape[0]
    tm = _tile(m, TOKEN_TILE)
    te = _tile(n_exp, EXPERT_TILE)
    n_tiles = n_exp // te
    once = dict(pipeline_mode=pl.Buffered(1))
    gate_spec = pl.BlockSpec((PEER_HEADS, PEER_KEYS, tm), lambda i, j: (0, 0, i), **once)
    return pl.pallas_call(
        functools.partial(_peer_dense_kernel, n_tiles=n_tiles),
        grid=(m // tm, n_tiles + 1),
        in_specs=[
            pl.BlockSpec((tm, d), lambda i, j: (i, 0), **once),
            pl.BlockSpec((1, N_MOD, d), lambda i, j: (mod_row_map(i), 0, 0)),
            pl.BlockSpec((d, tm), lambda i, j: (0, i), **once),
            gate_spec, gate_spec, gate_spec, gate_spec,
            pl.BlockSpec((te, d), lambda i, j: (jnp.minimum(j, n_tiles - 1), 0)),
            pl.BlockSpec((d, te), lambda i, j: (0, jnp.maximum(j - 1, 0))),
        ],
        out_specs=pl.BlockSpec((tm, d), lambda i, j: (i, 0)),
        out_shape=jax.ShapeDtypeStruct((m, d), F32),
        scratch_shapes=[pltpu.VMEM((d, tm), F32), pltpu.VMEM((te, tm), F32), pltpu.VMEM((te, tm), F32),
                        pltpu.VMEM((te, tm), BF16)],
        compiler_params=_cparams(("arbitrary", "arbitrary")),
        name=name,
    )(h, mod, hnt, lr, e1, rb, e2, u, vt)


def _rope_tables(seq):
    t = jnp.arange(seq)
    row = (t // GRID_W).astype(F32)
    col = (t % GRID_W).astype(F32)
    inv = ROPE_THETA ** (-jnp.arange(AXIS_DIM // 2, dtype=F32) / (AXIS_DIM // 2))
    ar, ac = row[:, None] * inv, col[:, None] * inv
    ang = jnp.concatenate([ar, ar, ac, ac], -1)
    sign = jnp.tile(jnp.concatenate([-jnp.ones(AXIS_DIM // 2, F32), jnp.ones(AXIS_DIM // 2, F32)]), 2)
    return jnp.cos(ang), jnp.sin(ang) * sign


def _gain_table(qn_a, kn_a, qn_b, kn_b, vn_c):
    scale = HEAD_DIM ** -0.5
    ones_a = jnp.ones((A_KV_WIDTH,), F32)
    ones_b = jnp.ones((B_KV_WIDTH,), F32)
    return jnp.concatenate([
        jnp.tile(qn_a * scale, A_HEADS), jnp.tile(kn_a, A_KV), ones_a,
        jnp.tile(qn_b * scale, B_HEADS), jnp.tile(kn_b, B_KV), ones_b,
        jnp.ones((C_WIDTH,), F32), vn_c,
    ]).reshape(1, IN_WIDTH)


def kernel(x, c, ctx, c_ctx, w_ada, b_ada, norm1_g, norm2_g, w_in, qn_a, kn_a, qn_b, kn_b, sink_b, vn_c, ws_c,
           bs_c, g_group, w_out, peer_wq, peer_keys, peer_u, peer_v):
    batch, seq, d = x.shape
    n_ctx = ctx.shape[1]
    depth = w_ada.shape[0]
    assert batch < MOD_ROWS and seq % TOKEN_TILE == 0 and TOKEN_TILE % n_ctx == 0
    assert seq % GRID_W == 0 and n_ctx % CHUNK == 0

    ctx_row = batch
    cc = jnp.zeros((MOD_ROWS, d), F32).at[:batch].set(c).at[ctx_row].set(c_ctx)
    mod_all = _ada_modulation(cc, w_ada, b_ada).reshape(depth, MOD_ROWS, N_MOD, d)

    cos, sin_signed = _rope_tables(seq)
    tiles_per_seq = seq // TOKEN_TILE
    lat_row = lambda i: i // tiles_per_seq
    ctx_row_map = lambda i: ctx_row
    no_sink = jnp.full((A_HEADS,), NEG_INF, F32)

    h = x.reshape(batch * seq, d)
    hc = ctx.reshape(batch * n_ctx, d)
    for l in range(depth):
        last = l == depth - 1
        mod = mod_all[l]
        w_in_l = w_in[l].astype(BF16)
        w_out_l = w_out[l].astype(BF16)
        gains = _gain_table(qn_a[l], kn_a[l], qn_b[l], kn_b[l], vn_c[l])
        n1 = norm1_g[l].reshape(1, d)
        n2 = norm2_g[l].reshape(1, d)
        ws = ws_c[l].astype(BF16)
        bs_b = jnp.broadcast_to(bs_c[l][:, :, None], (C_GROUPS, CHUNK, C_GROUP_DIM))
        gg = g_group[l].reshape(1, MIX_WIDTH)
        wq_t = peer_wq[l].T.astype(BF16)
        keys = peer_keys[l].reshape(2 * PEER_HEADS, PEER_KEYS, PEER_HALF).astype(BF16)
        u_tab = peer_u[l].astype(BF16)
        v_t = peer_v[l].T.astype(BF16)

        proj = _in_projection(h, mod, lat_row, n1, w_in_l, gains, cos, sin_signed,
                              tiles_per_seq=tiles_per_seq, use_rope=True)
        proj_c = _in_projection(hc, mod, ctx_row_map, n1, w_in_l, gains, cos, sin_signed,
                                tiles_per_seq=tiles_per_seq, use_rope=False)
        ya = _global_attention(proj, proj_c, batch, seq, n_ctx)
        yb = _window_attention(sink_b[l], proj, proj_c, batch, seq, n_ctx)
        h = _out_projection(h, mod, lat_row, ya, yb, proj, ws, bs_b, gg, w_out_l, "out_projection")
        routed = _peer_route(h, mod, lat_row, n2, wq_t, keys, "peer_route")
        h = _peer_dense(h, mod, lat_row, *routed, u_tab, v_t, "peer_dense")
        if not last:
            cya = _context_attention(no_sink, proj_c, batch, n_ctx, OFF_QA, OFF_KA, OFF_VA, "attn_ctx_a")
            cyb = _context_attention(sink_b[l], proj_c, batch, n_ctx, OFF_QB, OFF_KB, OFF_VB, "attn_ctx_b")
            hc = _out_projection(hc, mod, ctx_row_map, cya, cyb, proj_c, ws, bs_b, gg, w_out_l, "out_projection_ctx")
            routed_c = _peer_route(hc, mod, ctx_row_map, n2, wq_t, keys, "peer_route_ctx")
            hc = _peer_dense(hc, mod, ctx_row_map, *routed_c, u_tab, v_t, "peer_dense_ctx")
    return h.reshape(batch, seq, d)
```

```python
import functools

import jax
import jax.numpy as jnp
from jax import lax
from jax.experimental import pallas as pl
from jax.experimental.pallas import tpu as pltpu

F32 = jnp.float32
BF16 = jnp.bfloat16

GRID_W = 64
HEAD_DIM = 128
AXIS_DIM = HEAD_DIM // 2
ROPE_THETA = 10000.0
A_HEADS, A_KV = 6, 2
B_HEADS, B_KV = 6, 2
WINDOW = 128
C_GROUPS, C_GROUP_DIM, CHUNK = 4, 128, 128
A_WIDTH = A_HEADS * HEAD_DIM
B_WIDTH = B_HEADS * HEAD_DIM
C_WIDTH = C_GROUPS * C_GROUP_DIM
MIX_WIDTH = A_WIDTH + B_WIDTH + C_WIDTH
A_KV_WIDTH = A_KV * HEAD_DIM
B_KV_WIDTH = B_KV * HEAD_DIM
IN_WIDTH = A_WIDTH + 2 * A_KV_WIDTH + B_WIDTH + 2 * B_KV_WIDTH + 2 * C_WIDTH
GROUP = A_HEADS // A_KV
PEER_HEADS = 8
PEER_KEYS = 128
PEER_QDIM = 256
PEER_HALF = PEER_QDIM // 2
PEER_TOPK = 16
N_MOD = 6
EPS = 1e-6
NEG_INF = -1e30

OFF_QA = 0
OFF_KA = OFF_QA + A_WIDTH
OFF_VA = OFF_KA + A_KV_WIDTH
OFF_QB = OFF_VA + A_KV_WIDTH
OFF_KB = OFF_QB + B_WIDTH
OFF_VB = OFF_KB + B_KV_WIDTH
OFF_UC = OFF_VB + B_KV_WIDTH
OFF_VC = OFF_UC + C_WIDTH

MOD_ROWS = 16
TOKEN_TILE = 512
IN_COL_TILE = 256
OUT_COL_TILE = 512
ADA_COL_TILE = 1024
Q_TILE_GLOBAL = 256
Q_TILE_WINDOW = 128
EXPERT_TILE = 1024
LANE = 128
SUBLANE = 8
VMEM_LIMIT = 56 * 1024 * 1024


def _tile(n, preferred):
    t = min(preferred, n)
    while n % t:
        t -= LANE
    return t


def _cparams(sem):
    return pltpu.CompilerParams(dimension_semantics=sem, vmem_limit_bytes=VMEM_LIMIT)


def _rms_scale(x):
    return lax.rsqrt(jnp.mean(x * x, -1, keepdims=True) + EPS)


def _ada_kernel(c_ref, w_ref, b_ref, o_ref):
    c = c_ref[...]
    a = (c * jax.nn.sigmoid(c)).astype(BF16)
    o_ref[0] = jnp.dot(a, w_ref[0].astype(BF16), preferred_element_type=F32) + b_ref[0]


def _ada_modulation(cc, w_ada, b_ada):
    depth, d, n = w_ada.shape
    tn = _tile(n, ADA_COL_TILE)
    return pl.pallas_call(
        _ada_kernel,
        grid=(depth, n // tn),
        in_specs=[
            pl.BlockSpec((MOD_ROWS, d), lambda l, j: (0, 0)),
            pl.BlockSpec((1, d, tn), lambda l, j: (l, 0, j)),
            pl.BlockSpec((1, 1, tn), lambda l, j: (l, 0, j)),
        ],
        out_specs=pl.BlockSpec((1, MOD_ROWS, tn), lambda l, j: (l, 0, j)),
        out_shape=jax.ShapeDtypeStruct((depth, MOD_ROWS, n), F32),
        compiler_params=_cparams(("arbitrary", "arbitrary")),
        name="ada_modulation",
    )(cc, w_ada, b_ada.reshape(depth, 1, n))


def _col_tile_kinds():
    kinds = []
    for j in range(IN_WIDTH // IN_COL_TILE):
        c = j * IN_COL_TILE
        if c < OFF_VA or OFF_QB <= c < OFF_VB:
            kinds.append("head")
        elif c < OFF_QB or c < OFF_UC:
            kinds.append("plain")
        elif c < OFF_VC:
            kinds.append("gelu")
        else:
            kinds.append("gelu_norm")
    return kinds


def _inproj_kernel(h_ref, mod_ref, g_ref, w_ref, gain_ref, cos_ref, sin_ref, o_ref, *, use_rope):
    x = h_ref[...]
    xn = x * _rms_scale(x) * g_ref[...]
    xn = (xn * (1.0 + mod_ref[0, 1:2, :]) + mod_ref[0, 0:1, :]).astype(BF16)

    def rope(a):
        if not use_rope:
            return a
        lane = lax.broadcasted_iota(jnp.int32, a.shape, 1)
        first = (lane // (AXIS_DIM // 2)) % 2 == 0
        rot = jnp.where(first, pltpu.roll(a, HEAD_DIM - AXIS_DIM // 2, 1), pltpu.roll(a, AXIS_DIM // 2, 1))
        return a * cos_ref[...] + rot * sin_ref[...]

    def gelu_norm(a, g):
        a = jax.nn.gelu(a)
        return a * _rms_scale(a) * g

    epilogue = {
        "head": lambda a, g: rope(a * _rms_scale(a) * g),
        "plain": lambda a, g: a,
        "gelu": lambda a, g: jax.nn.gelu(a),
        "gelu_norm": gelu_norm,
    }
    for j, kind in enumerate(_col_tile_kinds()):
        c0 = j * IN_COL_TILE
        acc = jnp.dot(xn, w_ref[:, c0:c0 + IN_COL_TILE], preferred_element_type=F32)
        for k in range(IN_COL_TILE // LANE):
            lo = c0 + k * LANE
            part = epilogue[kind](acc[:, k * LANE:(k + 1) * LANE], gain_ref[:, lo:lo + LANE])
            o_ref[:, lo:lo + LANE] = part.astype(o_ref.dtype)


def _in_projection(h, mod, mod_row_map, norm_g, w_in, gains, cos, sin_signed, *, tiles_per_seq, use_rope):
    m, d = h.shape
    tm = _tile(m, TOKEN_TILE)
    if use_rope:
        pos_map = lambda i: (i % tiles_per_seq, 0)
    else:
        pos_map = lambda i: (0, 0)
    return pl.pallas_call(
        functools.partial(_inproj_kernel, use_rope=use_rope),
        grid=(m // tm,),
        in_specs=[
            pl.BlockSpec((tm, d), lambda i: (i, 0)),
            pl.BlockSpec((1, N_MOD, d), lambda i: (mod_row_map(i), 0, 0)),
            pl.BlockSpec((1, d), lambda i: (0, 0)),
            pl.BlockSpec((d, IN_WIDTH), lambda i: (0, 0)),
            pl.BlockSpec((1, IN_WIDTH), lambda i: (0, 0)),
            pl.BlockSpec((tm, HEAD_DIM), pos_map),
            pl.BlockSpec((tm, HEAD_DIM), pos_map),
        ],
        out_specs=pl.BlockSpec((tm, IN_WIDTH), lambda i: (i, 0)),
        out_shape=jax.ShapeDtypeStruct((m, IN_WIDTH), BF16),
        compiler_params=_cparams(("arbitrary",)),
        name="in_projection_rope" if use_rope else "in_projection_ctx",
    )(h, mod, norm_g, w_in, gains, cos, sin_signed)


_NT = (((1,), (1,)), ((), ()))


def _stack_heads(q_refs):
    return jnp.concatenate([r[...] for r in q_refs], 0)


def _q_specs(rows, row_fn, off_q):
    def spec(g):
        return pl.BlockSpec((rows, HEAD_DIM),
                            lambda *ids: (row_fn(*ids), off_q // HEAD_DIM + ids[1] * GROUP + g))
    return [spec(g) for g in range(GROUP)]


def _unstack_heads(o, tq):
    return jnp.concatenate([o[g * tq:(g + 1) * tq] for g in range(GROUP)], 1)


def _sink_column(sink_ref, kv, tq):
    row = lax.broadcasted_iota(jnp.int32, (GROUP * tq, 1), 0)
    col = jnp.full((GROUP * tq, 1), sink_ref[kv * GROUP + GROUP - 1], F32)
    for g in range(GROUP - 2, -1, -1):
        col = jnp.where(row < (g + 1) * tq, sink_ref[kv * GROUP + g], col)
    return col


def _attn_global_kernel(q0_ref, q1_ref, q2_ref, kc_ref, vc_ref, kl_ref, vl_ref, o_ref):
    tq = q0_ref.shape[0]
    qs = _stack_heads((q0_ref, q1_ref, q2_ref))
    sc = lax.dot_general(qs, kc_ref[...], _NT, preferred_element_type=F32)
    sl = lax.dot_general(qs, kl_ref[...], _NT, preferred_element_type=F32)
    m = jnp.maximum(jnp.max(sc, -1, keepdims=True), jnp.max(sl, -1, keepdims=True))
    pc = jnp.exp(sc - m)
    pw = jnp.exp(sl - m)
    denom = jnp.sum(pc, -1, keepdims=True) + jnp.sum(pw, -1, keepdims=True)
    o = (jnp.dot(pc.astype(BF16), vc_ref[...], preferred_element_type=F32)
         + jnp.dot(pw.astype(BF16), vl_ref[...], preferred_element_type=F32))
    o_ref[...] = _unstack_heads(o / denom, tq).astype(o_ref.dtype)


def _global_attention(proj, proj_ctx, batch, seq, n_ctx):
    tq = _tile(seq, Q_TILE_GLOBAL)
    nq = seq // tq
    qw = GROUP * HEAD_DIM
    return pl.pallas_call(
        _attn_global_kernel,
        grid=(batch, A_KV, nq),
        in_specs=[
            *_q_specs(tq, lambda b, k, i: b * nq + i, OFF_QA),
            pl.BlockSpec((n_ctx, HEAD_DIM), lambda b, k, i: (b, OFF_KA // HEAD_DIM + k)),
            pl.BlockSpec((n_ctx, HEAD_DIM), lambda b, k, i: (b, OFF_VA // HEAD_DIM + k)),
            pl.BlockSpec((seq, HEAD_DIM), lambda b, k, i: (b, OFF_KA // HEAD_DIM + k)),
            pl.BlockSpec((seq, HEAD_DIM), lambda b, k, i: (b, OFF_VA // HEAD_DIM + k)),
        ],
        out_specs=pl.BlockSpec((tq, qw), lambda b, k, i: (b * nq + i, k)),
        out_shape=jax.ShapeDtypeStruct((batch * seq, A_WIDTH), BF16),
        compiler_params=_cparams(("arbitrary", "arbitrary", "arbitrary")),
        name="attn_global",
    )(proj, proj, proj, proj_ctx, proj_ctx, proj, proj)


def _attn_window_kernel(sink_ref, q0_ref, q1_ref, q2_ref, kc_ref, vc_ref, kl_ref, vl_ref, o_ref):
    tq = q0_ref.shape[0]
    seq = kl_ref.shape[0]
    span = tq + 2 * WINDOW
    kv = pl.program_id(1)
    i = pl.program_id(2)
    start = pl.multiple_of(jnp.clip(i * tq - WINDOW, 0, seq - span), LANE)
    qs = _stack_heads((q0_ref, q1_ref, q2_ref))
    sw = lax.dot_general(qs, kl_ref[pl.ds(start, span), :], _NT, preferred_element_type=F32)
    qpos = i * tq + lax.broadcasted_iota(jnp.int32, (tq, span), 0)
    kpos = start + lax.broadcasted_iota(jnp.int32, (tq, span), 1)
    valid = jnp.abs(kpos - qpos) <= WINDOW
    sw = jnp.where(jnp.concatenate([valid] * GROUP, 0), sw, NEG_INF)
    sc = lax.dot_general(qs, kc_ref[...], _NT, preferred_element_type=F32)
    ss = _sink_column(sink_ref, kv, tq)
    m = jnp.maximum(jnp.maximum(jnp.max(sw, -1, keepdims=True), jnp.max(sc, -1, keepdims=True)), ss)
    pw = jnp.exp(sw - m)
    pc = jnp.exp(sc - m)
    denom = jnp.sum(pw, -1, keepdims=True) + jnp.sum(pc, -1, keepdims=True) + jnp.exp(ss - m)
    o = (jnp.dot(pw.astype(BF16), vl_ref[pl.ds(start, span), :], preferred_element_type=F32)
         + jnp.dot(pc.astype(BF16), vc_ref[...], preferred_element_type=F32))
    o_ref[...] = _unstack_heads(o / denom, tq).astype(o_ref.dtype)


def _window_attention(sink, proj, proj_ctx, batch, seq, n_ctx):
    tq = Q_TILE_WINDOW
    nq = seq // tq
    qw = GROUP * HEAD_DIM
    return pl.pallas_call(
        _attn_window_kernel,
        grid=(batch, B_KV, nq),
        in_specs=[
            pl.BlockSpec(memory_space=pltpu.SMEM),
            *_q_specs(tq, lambda b, k, i: b * nq + i, OFF_QB),
            pl.BlockSpec((n_ctx, HEAD_DIM), lambda b, k, i: (b, OFF_KB // HEAD_DIM + k)),
            pl.BlockSpec((n_ctx, HEAD_DIM), lambda b, k, i: (b, OFF_VB // HEAD_DIM + k)),
            pl.BlockSpec((seq, HEAD_DIM), lambda b, k, i: (b, OFF_KB // HEAD_DIM + k)),
            pl.BlockSpec((seq, HEAD_DIM), lambda b, k, i: (b, OFF_VB // HEAD_DIM + k)),
        ],
        out_specs=pl.BlockSpec((tq, qw), lambda b, k, i: (b * nq + i, k)),
        out_shape=jax.ShapeDtypeStruct((batch * seq, B_WIDTH), BF16),
        compiler_params=_cparams(("arbitrary", "arbitrary", "arbitrary")),
        name="attn_window",
    )(sink, proj, proj, proj, proj_ctx, proj_ctx, proj, proj)


def _attn_ctx_kernel(sink_ref, q0_ref, q1_ref, q2_ref, k_ref, v_ref, o_ref):
    tq = q0_ref.shape[0]
    kv = pl.program_id(1)
    qs = _stack_heads((q0_ref, q1_ref, q2_ref))
    s = lax.dot_general(qs, k_ref[...], _NT, preferred_element_type=F32)
    ss = _sink_column(sink_ref, kv, tq)
    m = jnp.maximum(jnp.max(s, -1, keepdims=True), ss)
    p = jnp.exp(s - m)
    denom = jnp.sum(p, -1, keepdims=True) + jnp.exp(ss - m)
    o = jnp.dot(p.astype(BF16), v_ref[...], preferred_element_type=F32)
    o_ref[...] = _unstack_heads(o / denom, tq).astype(o_ref.dtype)


def _context_attention(sink, proj_ctx, batch, n_ctx, off_q, off_k, off_v, name):
    qw = GROUP * HEAD_DIM
    n_kv = A_KV
    return pl.pallas_call(
        _attn_ctx_kernel,
        grid=(batch, n_kv),
        in_specs=[
            pl.BlockSpec(memory_space=pltpu.SMEM),
            *_q_specs(n_ctx, lambda b, k: b, off_q),
            pl.BlockSpec((n_ctx, HEAD_DIM), lambda b, k: (b, off_k // HEAD_DIM + k)),
            pl.BlockSpec((n_ctx, HEAD_DIM), lambda b, k: (b, off_v // HEAD_DIM + k)),
        ],
        out_specs=pl.BlockSpec((n_ctx, qw), lambda b, k: (b, k)),
        out_shape=jax.ShapeDtypeStruct((batch * n_ctx, n_kv * qw), BF16),
        compiler_params=_cparams(("arbitrary", "arbitrary")),
        name=name,
    )(sink, proj_ctx, proj_ctx, proj_ctx, proj_ctx, proj_ctx)


def _outproj_kernel(h_ref, mod_ref, ya_ref, yb_ref, u_ref, v_ref, ws_ref, bs_ref, gg_ref, w_ref, o_ref, y_ref):
    j = pl.program_id(1)

    @pl.when(j == 0)
    def _():
        tm = ya_ref.shape[0]
        u = u_ref[...]
        v = v_ref[...]
        rows = []
        for c in range(tm // CHUNK):
            r0 = c * CHUNK
            cols = []
            for g in range(C_GROUPS):
                c0 = g * C_GROUP_DIM
                s = jnp.dot(ws_ref[g], v[r0:r0 + CHUNK, c0:c0 + C_GROUP_DIM],
                            preferred_element_type=F32) + bs_ref[g]
                cols.append(u[r0:r0 + CHUNK, c0:c0 + C_GROUP_DIM].astype(F32) * s)
            rows.append(jnp.concatenate(cols, 1))
        yc = jnp.concatenate(rows, 0)
        ya = ya_ref[...].astype(F32)
        yb = yb_ref[...].astype(F32)
        y = jnp.concatenate([ya * _rms_scale(ya), yb * _rms_scale(yb), yc * _rms_scale(yc)], 1) * gg_ref[...]
        y_ref[...] = y.astype(BF16)

    acc = jnp.dot(y_ref[...], w_ref[...], preferred_element_type=F32)
    o_ref[...] = h_ref[...] + mod_ref[0, 2:3, :] * acc


def _out_projection(h, mod, mod_row_map, ya, yb, proj, ws, bs_b, g_group, w_out, name):
    m, d = h.shape
    tm = _tile(m, TOKEN_TILE)
    tn = _tile(d, OUT_COL_TILE)
    return pl.pallas_call(
        _outproj_kernel,
        grid=(m // tm, d // tn),
        in_specs=[
            pl.BlockSpec((tm, tn), lambda i, j: (i, j)),
            pl.BlockSpec((1, N_MOD, tn), lambda i, j: (mod_row_map(i), 0, j)),
            pl.BlockSpec((tm, A_WIDTH), lambda i, j: (i, 0)),
            pl.BlockSpec((tm, B_WIDTH), lambda i, j: (i, 0)),
            pl.BlockSpec((tm, C_WIDTH), lambda i, j: (i, OFF_UC // C_WIDTH)),
            pl.BlockSpec((tm, C_WIDTH), lambda i, j: (i, OFF_VC // C_WIDTH)),
            pl.BlockSpec((C_GROUPS, CHUNK, CHUNK), lambda i, j: (0, 0, 0)),
            pl.BlockSpec((C_GROUPS, CHUNK, C_GROUP_DIM), lambda i, j: (0, 0, 0)),
            pl.BlockSpec((1, MIX_WIDTH), lambda i, j: (0, 0)),
            pl.BlockSpec((MIX_WIDTH, tn), lambda i, j: (0, j)),
        ],
        out_specs=pl.BlockSpec((tm, tn), lambda i, j: (i, j)),
        out_shape=jax.ShapeDtypeStruct((m, d), F32),
        scratch_shapes=[pltpu.VMEM((tm, MIX_WIDTH), BF16)],
        compiler_params=_cparams(("arbitrary", "arbitrary")),
        name=name,
    )(h, mod, ya, yb, proj, proj, ws, bs_b, g_group, w_out)


def _top16(s):
    nk, n = s.shape
    idx = lax.broadcasted_iota(jnp.int32, (nk, n), 0).astype(F32)
    row16 = lax.broadcasted_iota(jnp.int32, (PEER_TOPK, n), 0)
    vals = jnp.zeros((PEER_TOPK, n), F32)
    rank = jnp.full((nk, n), float(PEER_TOPK), F32)
    work = s
    for r in range(PEER_TOPK):
        mx = jnp.max(work, 0, keepdims=True)
        first = jnp.min(jnp.where(work == mx, idx, float(nk)), 0, keepdims=True)
        hit = idx == first
        rank = jnp.where(hit, float(r), rank)
        vals = jnp.where(row16 == r, mx, vals)
        work = jnp.where(hit, -jnp.inf, work)
    return vals, rank


def _staircase(v1, v2):
    n = v1.shape[1]
    k = PEER_TOPK
    half = k // 2
    slabs, cis = [], []
    r16 = lax.broadcasted_iota(jnp.int32, (k, n), 0)
    r8 = lax.broadcasted_iota(jnp.int32, (half, n), 0)
    slabs.append(v1 + v2[0:1])
    cis.append(r16 * k)
    for b in range(1, half):
        slabs.append(v1[0:half] + v2[b:b + 1])
        cis.append(r8 * k + b)
    slabs.append(v1[0:1] + v2[half:k])
    cis.append(r8 + half)
    cand = jnp.concatenate(slabs, 0)
    ci = jnp.concatenate(cis, 0).astype(F32)
    big = float(k * k)
    work = cand
    sel = jnp.zeros(cand.shape, jnp.bool_)
    for _ in range(k):
        mx = jnp.max(work, 0, keepdims=True)
        first = jnp.min(jnp.where(work == mx, ci, big), 0, keepdims=True)
        hit = ci == first
        sel = jnp.logical_or(sel, hit)
        work = jnp.where(hit, -jnp.inf, work)
    top = cand[0:1]
    z = jnp.sum(jnp.where(sel, jnp.exp(cand - top), 0.0), 0, keepdims=True)
    self = sel.astype(F32)
    lo = self[0:half]
    for b in range(1, half):
        lo = lo + self[k + (b - 1) * half:k + b * half]
    tail = jnp.sum(self[k + (half - 1) * half:], 0, keepdims=True)
    lo = lo + jnp.where(r8 == 0, tail, 0.0)
    counts = jnp.concatenate([lo, self[half:k]], 0)
    return counts, z


def _peer_route_kernel(h_ref, mod_ref, g_ref, wqt_ref, keys_ref, hnt_ref, lr_ref, e1_ref, rb_ref, e2_ref, qt_ref):
    hd = pl.program_id(1)
    tm = h_ref.shape[0]

    @pl.when(hd == 0)
    def _():
        x = h_ref[...]
        hn = x * _rms_scale(x) * g_ref[...]
        hn = hn * (1.0 + mod_ref[0, 4:5, :]) + mod_ref[0, 3:4, :]
        hnt = hn.T.astype(BF16)
        hnt_ref[...] = hnt
        qt_ref[...] = jnp.dot(wqt_ref[...], hnt, preferred_element_type=F32).astype(BF16)

    q0 = pl.multiple_of(hd * PEER_QDIM, PEER_QDIM)
    k1 = keys_ref[2 * hd]
    k2 = keys_ref[2 * hd + 1]

    for c in range(tm // LANE):
        t0 = c * LANE
        s1 = jnp.dot(k1, qt_ref[pl.ds(q0, PEER_HALF), t0:t0 + LANE], preferred_element_type=F32)
        s2 = jnp.dot(k2, qt_ref[pl.ds(q0 + PEER_HALF, PEER_HALF), t0:t0 + LANE], preferred_element_type=F32)
        v1, rank1 = _top16(s1)
        v2, rank2 = _top16(s2)
        counts, z = _staircase(v1, v2)
        lr = jnp.zeros_like(rank1)
        for a in range(PEER_TOPK):
            lr = jnp.where(rank1 == float(a), counts[a:a + 1], lr)
        lr_ref[0, :, t0:t0 + LANE] = lr
        e1_ref[0, :, t0:t0 + LANE] = jnp.exp(s1 - v1[0:1]) / z
        rb_ref[0, :, t0:t0 + LANE] = rank2.astype(rb_ref.dtype)
        e2_ref[0, :, t0:t0 + LANE] = jnp.exp(s2 - v2[0:1]).astype(e2_ref.dtype)


def _peer_route(h, mod, mod_row_map, norm_g, wq_t, keys, name):
    m, d = h.shape
    tm = _tile(m, TOKEN_TILE)
    qd = wq_t.shape[0]
    gate_spec = pl.BlockSpec((1, PEER_KEYS, tm), lambda i, hd: (hd, 0, i))
    gate_shape = jax.ShapeDtypeStruct((PEER_HEADS, PEER_KEYS, m), F32)
    gate_shape_bf16 = jax.ShapeDtypeStruct((PEER_HEADS, PEER_KEYS, m), BF16)
    return pl.pallas_call(
        _peer_route_kernel,
        grid=(m // tm, PEER_HEADS),
        in_specs=[
            pl.BlockSpec((tm, d), lambda i, hd: (i, 0)),
            pl.BlockSpec((1, N_MOD, d), lambda i, hd: (mod_row_map(i), 0, 0)),
            pl.BlockSpec((1, d), lambda i, hd: (0, 0)),
            pl.BlockSpec((qd, d), lambda i, hd: (0, 0)),
            pl.BlockSpec((2 * PEER_HEADS, PEER_KEYS, PEER_HALF), lambda i, hd: (0, 0, 0)),
        ],
        out_specs=[pl.BlockSpec((d, tm), lambda i, hd: (0, i)), gate_spec, gate_spec, gate_spec, gate_spec],
        out_shape=[jax.ShapeDtypeStruct((d, m), BF16), gate_shape, gate_shape, gate_shape_bf16, gate_shape_bf16],
        scratch_shapes=[pltpu.VMEM((qd, tm), BF16)],
        compiler_params=_cparams(("arbitrary", "arbitrary")),
        name=name,
    )(h, mod, norm_g, wq_t, keys)


def _peer_dense_kernel(hnt_ref, lr_ref, e1_ref, rb_ref, e2_ref, u_ref, vt_ref, o_ref,
                       hid_even, hid_odd, a_ref, *, n_tiles):
    j = pl.program_id(1)
    te, tm = hid_even.shape
    per_step = te // PEER_KEYS

    def project(hid_ref):
        hid_ref[...] = jnp.dot(u_ref[...], hnt_ref[...], preferred_element_type=F32)

    def finish(hid_ref, jj):
        i1_base = pl.multiple_of(jj * per_step, SUBLANE)
        for c in range(tm // LANE):
            t0 = c * LANE
            lr_rows = [lr_ref[hd, pl.ds(i1_base, per_step), t0:t0 + LANE] for hd in range(PEER_HEADS)]
            e1_rows = [e1_ref[hd, pl.ds(i1_base, per_step), t0:t0 + LANE] for hd in range(PEER_HEADS)]
            for il in range(per_step):
                r0 = il * PEER_KEYS
                gate = jnp.zeros((PEER_KEYS, LANE), BF16)
                for hd in range(PEER_HEADS):
                    lr = jnp.broadcast_to(lr_rows[hd][il:il + 1], (PEER_KEYS, LANE)).astype(BF16)
                    e1 = jnp.broadcast_to(e1_rows[hd][il:il + 1], (PEER_KEYS, LANE)).astype(BF16)
                    sel = rb_ref[hd, :, t0:t0 + LANE] < lr
                    gate = gate + jnp.where(sel, e2_ref[hd, :, t0:t0 + LANE], jnp.zeros_like(e1)) * e1
                act = jax.nn.gelu(hid_ref[r0:r0 + PEER_KEYS, t0:t0 + LANE])
                a_ref[r0:r0 + PEER_KEYS, t0:t0 + LANE] = act.astype(BF16) * gate
        o_ref[...] += jnp.dot(vt_ref[0], a_ref[...], preferred_element_type=F32)

    @pl.when(j == 0)
    def _():
        o_ref[...] = jnp.zeros_like(o_ref)
        project(hid_even)

    middle = jnp.logical_and(j > 0, j < n_tiles)

    @pl.when(jnp.logical_and(middle, j % 2 == 1))
    def _():
        project(hid_odd)
        finish(hid_even, j - 1)

    @pl.when(jnp.logical_and(middle, j % 2 == 0))
    def _():
        project(hid_even)
        finish(hid_odd, j - 1)

    @pl.when(j == n_tiles)
    def _():
        finish(hid_odd if n_tiles % 2 == 0 else hid_even, n_tiles - 1)


def _peer_dense(hnt, lr, e1, rb, e2, u, vt_tiles, name):
    d, m = hnt.shape
    n_tiles, _, te = vt_tiles.shape
    tm = _tile(m, TOKEN_TILE)
    gate_spec = pl.BlockSpec((PEER_HEADS, PEER_KEYS, tm), lambda i, j: (0, 0, i))
    return pl.pallas_call(
        functools.partial(_peer_dense_kernel, n_tiles=n_tiles),
        grid=(m // tm, n_tiles + 1),
        in_specs=[
            pl.BlockSpec((d, tm), lambda i, j: (0, i)),
            gate_spec, gate_spec, gate_spec, gate_spec,
            pl.BlockSpec((te, d), lambda i, j: (jnp.minimum(j, n_tiles - 1), 0)),
            pl.BlockSpec((1, d, te), lambda i, j: (jnp.maximum(j - 1, 0), 0, 0)),
        ],
        out_specs=pl.BlockSpec((d, tm), lambda i, j: (0, i)),
        out_shape=jax.ShapeDtypeStruct((d, m), F32),
        scratch_shapes=[pltpu.VMEM((te, tm), F32), pltpu.VMEM((te, tm), F32), pltpu.VMEM((te, tm), BF16)],
        compiler_params=_cparams(("arbitrary", "arbitrary")),
        name=name,
    )(hnt, lr, e1, rb, e2, u, vt_tiles)


def _peer_residual_kernel(h_ref, mod_ref, ot_ref, o_ref):
    o_ref[...] = h_ref[...] + mod_ref[0, 5:6, :] * ot_ref[...].T


def _peer_residual(h, mod, mod_row_map, out_t, name):
    m, d = h.shape
    tm = _tile(m, TOKEN_TILE)
    return pl.pallas_call(
        _peer_residual_kernel,
        grid=(m // tm,),
        in_specs=[
            pl.BlockSpec((tm, d), lambda i: (i, 0)),
            pl.BlockSpec((1, N_MOD, d), lambda i: (mod_row_map(i), 0, 0)),
            pl.BlockSpec((d, tm), lambda i: (0, i)),
        ],
        out_specs=pl.BlockSpec((tm, d), lambda i: (i, 0)),
        out_shape=jax.ShapeDtypeStruct((m, d), F32),
        compiler_params=_cparams(("arbitrary",)),
        name=name,
    )(h, mod, out_t)


def _rope_tables(seq):
    t = jnp.arange(seq)
    row = (t // GRID_W).astype(F32)
    col = (t % GRID_W).astype(F32)
    inv = ROPE_THETA ** (-jnp.arange(AXIS_DIM // 2, dtype=F32) / (AXIS_DIM // 2))
    ar, ac = row[:, None] * inv, col[:, None] * inv
    ang = jnp.concatenate([ar, ar, ac, ac], -1)
    sign = jnp.tile(jnp.concatenate([-jnp.ones(AXIS_DIM // 2, F32), jnp.ones(AXIS_DIM // 2, F32)]), 2)
    return jnp.cos(ang), jnp.sin(ang) * sign


def _gain_table(qn_a, kn_a, qn_b, kn_b, vn_c):
    scale = HEAD_DIM ** -0.5
    ones_a = jnp.ones((A_KV_WIDTH,), F32)
    ones_b = jnp.ones((B_KV_WIDTH,), F32)
    return jnp.concatenate([
        jnp.tile(qn_a * scale, A_HEADS), jnp.tile(kn_a, A_KV), ones_a,
        jnp.tile(qn_b * scale, B_HEADS), jnp.tile(kn_b, B_KV), ones_b,
        jnp.ones((C_WIDTH,), F32), vn_c,
    ]).reshape(1, IN_WIDTH)


def kernel(x, c, ctx, c_ctx, w_ada, b_ada, norm1_g, norm2_g, w_in, qn_a, kn_a, qn_b, kn_b, sink_b, vn_c, ws_c,
           bs_c, g_group, w_out, peer_wq, peer_keys, peer_u, peer_v):
    batch, seq, d = x.shape
    n_ctx = ctx.shape[1]
    depth = w_ada.shape[0]
    assert batch < MOD_ROWS and seq % TOKEN_TILE == 0 and TOKEN_TILE % n_ctx == 0
    assert seq % GRID_W == 0 and n_ctx % CHUNK == 0

    ctx_row = batch
    cc = jnp.zeros((MOD_ROWS, d), F32).at[:batch].set(c).at[ctx_row].set(c_ctx)
    mod_all = _ada_modulation(cc, w_ada, b_ada).reshape(depth, MOD_ROWS, N_MOD, d)

    cos, sin_signed = _rope_tables(seq)
    tiles_per_seq = seq // TOKEN_TILE
    lat_row = lambda i: i // tiles_per_seq
    ctx_row_map = lambda i: ctx_row
    no_sink = jnp.full((A_HEADS,), NEG_INF, F32)

    h = x.reshape(batch * seq, d)
    hc = ctx.reshape(batch * n_ctx, d)
    for l in range(depth):
        last = l == depth - 1
        mod = mod_all[l]
        w_in_l = w_in[l].astype(BF16)
        w_out_l = w_out[l].astype(BF16)
        gains = _gain_table(qn_a[l], kn_a[l], qn_b[l], kn_b[l], vn_c[l])
        n1 = norm1_g[l].reshape(1, d)
        n2 = norm2_g[l].reshape(1, d)
        ws = ws_c[l].astype(BF16)
        bs_b = jnp.broadcast_to(bs_c[l][:, :, None], (C_GROUPS, CHUNK, C_GROUP_DIM))
        gg = g_group[l].reshape(1, MIX_WIDTH)
        wq_t = peer_wq[l].T.astype(BF16)
        keys = peer_keys[l].reshape(2 * PEER_HEADS, PEER_KEYS, PEER_HALF).astype(BF16)
        u_tab = peer_u[l].astype(BF16)
        n_exp = peer_v.shape[1]
        te = _tile(n_exp, EXPERT_TILE)
        v_tiles = peer_v[l].astype(BF16).reshape(n_exp // te, te, d).transpose(0, 2, 1)

        proj = _in_projection(h, mod, lat_row, n1, w_in_l, gains, cos, sin_signed,
                              tiles_per_seq=tiles_per_seq, use_rope=True)
        proj_c = _in_projection(hc, mod, ctx_row_map, n1, w_in_l, gains, cos, sin_signed,
                                tiles_per_seq=tiles_per_seq, use_rope=False)
        ya = _global_attention(proj, proj_c, batch, seq, n_ctx)
        yb = _window_attention(sink_b[l], proj, proj_c, batch, seq, n_ctx)
        h = _out_projection(h, mod, lat_row, ya, yb, proj, ws, bs_b, gg, w_out_l, "out_projection")
        routed = _peer_route(h, mod, lat_row, n2, wq_t, keys, "peer_route")
        h = _peer_residual(h, mod, lat_row, _peer_dense(*routed, u_tab, v_tiles, "peer_dense"), "peer_residual")
        if not last:
            cya = _context_attention(no_sink, proj_c, batch, n_ctx, OFF_QA, OFF_KA, OFF_VA, "attn_ctx_a")
            cyb = _context_attention(sink_b[l], proj_c, batch, n_ctx, OFF_QB, OFF_KB, OFF_VB, "attn_ctx_b")
            hc = _out_projection(hc, mod, ctx_row_map, cya, cyb, proj_c, ws, bs_b, gg, w_out_l, "out_projection_ctx")
            routed_c = _peer_route(hc, mod, ctx_row_map, n2, wq_t, keys, "peer_route_ctx")
            hc = _peer_residual(hc, mod, ctx_row_map, _peer_dense(*routed_c, u_tab, v_tiles, "peer_dense_ctx"),
                                "peer_residual_ctx")
    return h.reshape(batch, seq, d)
```

```python
import functools

import jax
import jax.numpy as jnp
from jax import lax
from jax.experimental import pallas as pl
from jax.experimental.pallas import tpu as pltpu

F32 = jnp.float32
BF16 = jnp.bfloat16

GRID_W = 64
HEAD_DIM = 128
AXIS_DIM = HEAD_DIM // 2
ROPE_THETA = 10000.0
A_HEADS, A_KV = 6, 2
B_HEADS, B_KV = 6, 2
WINDOW = 128
C_GROUPS, C_GROUP_DIM, CHUNK = 4, 128, 128
A_WIDTH = A_HEADS * HEAD_DIM
B_WIDTH = B_HEADS * HEAD_DIM
C_WIDTH = C_GROUPS * C_GROUP_DIM
MIX_WIDTH = A_WIDTH + B_WIDTH + C_WIDTH
A_KV_WIDTH = A_KV * HEAD_DIM
B_KV_WIDTH = B_KV * HEAD_DIM
IN_WIDTH = A_WIDTH + 2 * A_KV_WIDTH + B_WIDTH + 2 * B_KV_WIDTH + 2 * C_WIDTH
GROUP = A_HEADS // A_KV
PEER_HEADS = 8
PEER_KEYS = 128
PEER_QDIM = 256
PEER_HALF = PEER_QDIM // 2
PEER_TOPK = 16
N_MOD = 6
EPS = 1e-6
NEG_INF = -1e30

OFF_QA = 0
OFF_KA = OFF_QA + A_WIDTH
OFF_VA = OFF_KA + A_KV_WIDTH
OFF_QB = OFF_VA + A_KV_WIDTH
OFF_KB = OFF_QB + B_WIDTH
OFF_VB = OFF_KB + B_KV_WIDTH
OFF_UC = OFF_VB + B_KV_WIDTH
OFF_VC = OFF_UC + C_WIDTH

MOD_ROWS = 16
TOKEN_TILE = 512
IN_COL_TILE = 256
OUT_COL_TILE = 512
ADA_COL_TILE = 1024
Q_TILE_GLOBAL = 256
Q_TILE_WINDOW = 128
EXPERT_TILE = 1024
LANE = 128
SUBLANE = 8
VMEM_LIMIT = 56 * 1024 * 1024


def _tile(n, preferred):
    t = min(preferred, n)
    while n % t:
        t -= LANE
    return t


def _cparams(sem):
    return pltpu.CompilerParams(dimension_semantics=sem, vmem_limit_bytes=VMEM_LIMIT)


def _rms_scale(x):
    return lax.rsqrt(jnp.mean(x * x, -1, keepdims=True) + EPS)


def _ada_kernel(c_ref, w_ref, b_ref, o_ref):
    c = c_ref[...]
    a = (c * jax.nn.sigmoid(c)).astype(BF16)
    o_ref[0] = jnp.dot(a, w_ref[0].astype(BF16), preferred_element_type=F32) + b_ref[0]


def _ada_modulation(cc, w_ada, b_ada):
    depth, d, n = w_ada.shape
    tn = _tile(n, ADA_COL_TILE)
    return pl.pallas_call(
        _ada_kernel,
        grid=(depth, n // tn),
        in_specs=[
            pl.BlockSpec((MOD_ROWS, d), lambda l, j: (0, 0)),
            pl.BlockSpec((1, d, tn), lambda l, j: (l, 0, j)),
            pl.BlockSpec((1, 1, tn), lambda l, j: (l, 0, j)),
        ],
        out_specs=pl.BlockSpec((1, MOD_ROWS, tn), lambda l, j: (l, 0, j)),
        out_shape=jax.ShapeDtypeStruct((depth, MOD_ROWS, n), F32),
        compiler_params=_cparams(("arbitrary", "arbitrary")),
        name="ada_modulation",
    )(cc, w_ada, b_ada.reshape(depth, 1, n))


def _col_tile_kinds():
    kinds = []
    for j in range(IN_WIDTH // IN_COL_TILE):
        c = j * IN_COL_TILE
        if c < OFF_VA or OFF_QB <= c < OFF_VB:
            kinds.append("head")
        elif c < OFF_QB or c < OFF_UC:
            kinds.append("plain")
        elif c < OFF_VC:
            kinds.append("gelu")
        else:
            kinds.append("gelu_norm")
    return kinds


def _inproj_kernel(h_ref, mod_ref, g_ref, w_ref, gain_ref, cos_ref, sin_ref, o_ref, *, use_rope):
    x = h_ref[...]
    xn = x * _rms_scale(x) * g_ref[...]
    xn = (xn * (1.0 + mod_ref[0, 1:2, :]) + mod_ref[0, 0:1, :]).astype(BF16)

    def rope(a):
        if not use_rope:
            return a
        lane = lax.broadcasted_iota(jnp.int32, a.shape, 1)
        first = (lane // (AXIS_DIM // 2)) % 2 == 0
        rot = jnp.where(first, pltpu.roll(a, HEAD_DIM - AXIS_DIM // 2, 1), pltpu.roll(a, AXIS_DIM // 2, 1))
        return a * cos_ref[...] + rot * sin_ref[...]

    def gelu_norm(a, g):
        a = jax.nn.gelu(a)
        return a * _rms_scale(a) * g

    epilogue = {
        "head": lambda a, g: rope(a * _rms_scale(a) * g),
        "plain": lambda a, g: a,
        "gelu": lambda a, g: jax.nn.gelu(a),
        "gelu_norm": gelu_norm,
    }
    for j, kind in enumerate(_col_tile_kinds()):
        c0 = j * IN_COL_TILE
        acc = jnp.dot(xn, w_ref[:, c0:c0 + IN_COL_TILE], preferred_element_type=F32)
        for k in range(IN_COL_TILE // LANE):
            lo = c0 + k * LANE
            part = epilogue[kind](acc[:, k * LANE:(k + 1) * LANE], gain_ref[:, lo:lo + LANE])
            o_ref[:, lo:lo + LANE] = part.astype(o_ref.dtype)


def _in_projection(h, mod, mod_row_map, norm_g, w_in, gains, cos, sin_signed, *, tiles_per_seq, use_rope):
    m, d = h.shape
    tm = _tile(m, TOKEN_TILE)
    if use_rope:
        pos_map = lambda i: (i % tiles_per_seq, 0)
    else:
        pos_map = lambda i: (0, 0)
    return pl.pallas_call(
        functools.partial(_inproj_kernel, use_rope=use_rope),
        grid=(m // tm,),
        in_specs=[
            pl.BlockSpec((tm, d), lambda i: (i, 0)),
            pl.BlockSpec((1, N_MOD, d), lambda i: (mod_row_map(i), 0, 0)),
            pl.BlockSpec((1, d), lambda i: (0, 0)),
            pl.BlockSpec((d, IN_WIDTH), lambda i: (0, 0)),
            pl.BlockSpec((1, IN_WIDTH), lambda i: (0, 0)),
            pl.BlockSpec((tm, HEAD_DIM), pos_map),
            pl.BlockSpec((tm, HEAD_DIM), pos_map),
        ],
        out_specs=pl.BlockSpec((tm, IN_WIDTH), lambda i: (i, 0)),
        out_shape=jax.ShapeDtypeStruct((m, IN_WIDTH), BF16),
        compiler_params=_cparams(("arbitrary",)),
        name="in_projection_rope" if use_rope else "in_projection_ctx",
    )(h, mod, norm_g, w_in, gains, cos, sin_signed)


_NT = (((1,), (1,)), ((), ()))


def _stack_heads(q_refs):
    return jnp.concatenate([r[...] for r in q_refs], 0)


def _q_specs(rows, row_fn, off_q):
    def spec(g):
        return pl.BlockSpec((rows, HEAD_DIM),
                            lambda *ids: (row_fn(*ids), off_q // HEAD_DIM + ids[1] * GROUP + g))
    return [spec(g) for g in range(GROUP)]


def _unstack_heads(o, tq):
    return jnp.concatenate([o[g * tq:(g + 1) * tq] for g in range(GROUP)], 1)


def _sink_column(sink_ref, kv, tq):
    row = lax.broadcasted_iota(jnp.int32, (GROUP * tq, 1), 0)
    col = jnp.full((GROUP * tq, 1), sink_ref[kv * GROUP + GROUP - 1], F32)
    for g in range(GROUP - 2, -1, -1):
        col = jnp.where(row < (g + 1) * tq, sink_ref[kv * GROUP + g], col)
    return col


def _attn_global_kernel(q0_ref, q1_ref, q2_ref, kc_ref, vc_ref, kl_ref, vl_ref, o_ref):
    tq = q0_ref.shape[0]
    qs = _stack_heads((q0_ref, q1_ref, q2_ref))
    sc = lax.dot_general(qs, kc_ref[...], _NT, preferred_element_type=F32)
    sl = lax.dot_general(qs, kl_ref[...], _NT, preferred_element_type=F32)
    m = jnp.maximum(jnp.max(sc, -1, keepdims=True), jnp.max(sl, -1, keepdims=True))
    pc = jnp.exp(sc - m)
    pw = jnp.exp(sl - m)
    denom = jnp.sum(pc, -1, keepdims=True) + jnp.sum(pw, -1, keepdims=True)
    o = (jnp.dot(pc.astype(BF16), vc_ref[...], preferred_element_type=F32)
         + jnp.dot(pw.astype(BF16), vl_ref[...], preferred_element_type=F32))
    o_ref[...] = _unstack_heads(o / denom, tq).astype(o_ref.dtype)


def _global_attention(proj, proj_ctx, batch, seq, n_ctx):
    tq = _tile(seq, Q_TILE_GLOBAL)
    nq = seq // tq
    qw = GROUP * HEAD_DIM
    return pl.pallas_call(
        _attn_global_kernel,
        grid=(batch, A_KV, nq),
        in_specs=[
            *_q_specs(tq, lambda b, k, i: b * nq + i, OFF_QA),
            pl.BlockSpec((n_ctx, HEAD_DIM), lambda b, k, i: (b, OFF_KA // HEAD_DIM + k)),
            pl.BlockSpec((n_ctx, HEAD_DIM), lambda b, k, i: (b, OFF_VA // HEAD_DIM + k)),
            pl.BlockSpec((seq, HEAD_DIM), lambda b, k, i: (b, OFF_KA // HEAD_DIM + k)),
            pl.BlockSpec((seq, HEAD_DIM), lambda b, k, i: (b, OFF_VA // HEAD_DIM + k)),
        ],
        out_specs=pl.BlockSpec((tq, qw), lambda b, k, i: (b * nq + i, k)),
        out_shape=jax.ShapeDtypeStruct((batch * seq, A_WIDTH), BF16),
        compiler_params=_cparams(("arbitrary", "arbitrary", "arbitrary")),
        name="attn_global",
    )(proj, proj, proj, proj_ctx, proj_ctx, proj, proj)


def _attn_window_kernel(sink_ref, q0_ref, q1_ref, q2_ref, kc_ref, vc_ref, kl_ref, vl_ref, o_ref):
    tq = q0_ref.shape[0]
    seq = kl_ref.shape[0]
    span = tq + 2 * WINDOW
    kv = pl.program_id(1)
    i = pl.program_id(2)
    start = pl.multiple_of(jnp.clip(i * tq - WINDOW, 0, seq - span), LANE)
    qs = _stack_heads((q0_ref, q1_ref, q2_ref))
    sw = lax.dot_general(qs, kl_ref[pl.ds(start, span), :], _NT, preferred_element_type=F32)
    qpos = i * tq + lax.broadcasted_iota(jnp.int32, (tq, span), 0)
    kpos = start + lax.broadcasted_iota(jnp.int32, (tq, span), 1)
    valid = jnp.abs(kpos - qpos) <= WINDOW
    sw = jnp.where(jnp.concatenate([valid] * GROUP, 0), sw, NEG_INF)
    sc = lax.dot_general(qs, kc_ref[...], _NT, preferred_element_type=F32)
    ss = _sink_column(sink_ref, kv, tq)
    m = jnp.maximum(jnp.maximum(jnp.max(sw, -1, keepdims=True), jnp.max(sc, -1, keepdims=True)), ss)
    pw = jnp.exp(sw - m)
    pc = jnp.exp(sc - m)
    denom = jnp.sum(pw, -1, keepdims=True) + jnp.sum(pc, -1, keepdims=True) + jnp.exp(ss - m)
    o = (jnp.dot(pw.astype(BF16), vl_ref[pl.ds(start, span), :], preferred_element_type=F32)
         + jnp.dot(pc.astype(BF16), vc_ref[...], preferred_element_type=F32))
    o_ref[...] = _unstack_heads(o / denom, tq).astype(o_ref.dtype)


def _window_attention(sink, proj, proj_ctx, batch, seq, n_ctx):
    tq = Q_TILE_WINDOW
    nq = seq // tq
    qw = GROUP * HEAD_DIM
    return pl.pallas_call(
        _attn_window_kernel,
        grid=(batch, B_KV, nq),
        in_specs=[
            pl.BlockSpec(memory_space=pltpu.SMEM),
            *_q_specs(tq, lambda b, k, i: b * nq + i, OFF_QB),
            pl.BlockSpec((n_ctx, HEAD_DIM), lambda b, k, i: (b, OFF_KB // HEAD_DIM + k)),
            pl.BlockSpec((n_ctx, HEAD_DIM), lambda b, k, i: (b, OFF_VB // HEAD_DIM + k)),
            pl.BlockSpec((seq, HEAD_DIM), lambda b, k, i: (b, OFF_KB // HEAD_DIM + k)),
            pl.BlockSpec((seq, HEAD_DIM), lambda b, k, i: (b, OFF_VB // HEAD_DIM + k)),
        ],
        out_specs=pl.BlockSpec((tq, qw), lambda b, k, i: (b * nq + i, k)),
        out_shape=jax.ShapeDtypeStruct((batch * seq, B_WIDTH), BF16),
        compiler_params=_cparams(("arbitrary", "arbitrary", "arbitrary")),
        name="attn_window",
    )(sink, proj, proj, proj, proj_ctx, proj_ctx, proj, proj)


def _attn_ctx_kernel(sink_ref, q0_ref, q1_ref, q2_ref, k_ref, v_ref, o_ref):
    tq = q0_ref.shape[0]
    kv = pl.program_id(1)
    qs = _stack_heads((q0_ref, q1_ref, q2_ref))
    s = lax.dot_general(qs, k_ref[...], _NT, preferred_element_type=F32)
    ss = _sink_column(sink_ref, kv, tq)
    m = jnp.maximum(jnp.max(s, -1, keepdims=True), ss)
    p = jnp.exp(s - m)
    denom = jnp.sum(p, -1, keepdims=True) + jnp.exp(ss - m)
    o = jnp.dot(p.astype(BF16), v_ref[...], preferred_element_type=F32)
    o_ref[...] = _unstack_heads(o / denom, tq).astype(o_ref.dtype)


def _context_attention(sink, proj_ctx, batch, n_ctx, off_q, off_k, off_v, name):
    qw = GROUP * HEAD_DIM
    n_kv = A_KV
    return pl.pallas_call(
        _attn_ctx_kernel,
        grid=(batch, n_kv),
        in_specs=[
            pl.BlockSpec(memory_space=pltpu.SMEM),
            *_q_specs(n_ctx, lambda b, k: b, off_q),
            pl.BlockSpec((n_ctx, HEAD_DIM), lambda b, k: (b, off_k // HEAD_DIM + k)),
            pl.BlockSpec((n_ctx, HEAD_DIM), lambda b, k: (b, off_v // HEAD_DIM + k)),
        ],
        out_specs=pl.BlockSpec((n_ctx, qw), lambda b, k: (b, k)),
        out_shape=jax.ShapeDtypeStruct((batch * n_ctx, n_kv * qw), BF16),
        compiler_params=_cparams(("arbitrary", "arbitrary")),
        name=name,
    )(sink, proj_ctx, proj_ctx, proj_ctx, proj_ctx, proj_ctx)


def _outproj_kernel(h_ref, mod_ref, ya_ref, yb_ref, u_ref, v_ref, ws_ref, bs_ref, gg_ref, w_ref, o_ref, y_ref):
    j = pl.program_id(1)

    @pl.when(j == 0)
    def _():
        tm = ya_ref.shape[0]
        u = u_ref[...]
        v = v_ref[...]
        rows = []
        for c in range(tm // CHUNK):
            r0 = c * CHUNK
            cols = []
            for g in range(C_GROUPS):
                c0 = g * C_GROUP_DIM
                s = jnp.dot(ws_ref[g], v[r0:r0 + CHUNK, c0:c0 + C_GROUP_DIM],
                            preferred_element_type=F32) + bs_ref[g]
                cols.append(u[r0:r0 + CHUNK, c0:c0 + C_GROUP_DIM].astype(F32) * s)
            rows.append(jnp.concatenate(cols, 1))
        yc = jnp.concatenate(rows, 0)
        ya = ya_ref[...].astype(F32)
        yb = yb_ref[...].astype(F32)
        y = jnp.concatenate([ya * _rms_scale(ya), yb * _rms_scale(yb), yc * _rms_scale(yc)], 1) * gg_ref[...]
        y_ref[...] = y.astype(BF16)

    acc = jnp.dot(y_ref[...], w_ref[...], preferred_element_type=F32)
    o_ref[...] = h_ref[...] + mod_ref[0, 2:3, :] * acc


def _out_projection(h, mod, mod_row_map, ya, yb, proj, ws, bs_b, g_group, w_out, name):
    m, d = h.shape
    tm = _tile(m, TOKEN_TILE)
    tn = _tile(d, OUT_COL_TILE)
    return pl.pallas_call(
        _outproj_kernel,
        grid=(m // tm, d // tn),
        in_specs=[
            pl.BlockSpec((tm, tn), lambda i, j: (i, j)),
            pl.BlockSpec((1, N_MOD, tn), lambda i, j: (mod_row_map(i), 0, j)),
            pl.BlockSpec((tm, A_WIDTH), lambda i, j: (i, 0)),
            pl.BlockSpec((tm, B_WIDTH), lambda i, j: (i, 0)),
            pl.BlockSpec((tm, C_WIDTH), lambda i, j: (i, OFF_UC // C_WIDTH)),
            pl.BlockSpec((tm, C_WIDTH), lambda i, j: (i, OFF_VC // C_WIDTH)),
            pl.BlockSpec((C_GROUPS, CHUNK, CHUNK), lambda i, j: (0, 0, 0)),
            pl.BlockSpec((C_GROUPS, CHUNK, C_GROUP_DIM), lambda i, j: (0, 0, 0)),
            pl.BlockSpec((1, MIX_WIDTH), lambda i, j: (0, 0)),
            pl.BlockSpec((MIX_WIDTH, tn), lambda i, j: (0, j)),
        ],
        out_specs=pl.BlockSpec((tm, tn), lambda i, j: (i, j)),
        out_shape=jax.ShapeDtypeStruct((m, d), F32),
        scratch_shapes=[pltpu.VMEM((tm, MIX_WIDTH), BF16)],
        compiler_params=_cparams(("arbitrary", "arbitrary")),
        name=name,
    )(h, mod, ya, yb, proj, proj, ws, bs_b, g_group, w_out)


def _top16(s, exact):
    nk, n = s.shape
    idx = lax.broadcasted_iota(jnp.int32, (nk, n), 0).astype(F32)
    row16 = lax.broadcasted_iota(jnp.int32, (PEER_TOPK, n), 0)
    vals = jnp.zeros((PEER_TOPK, n), F32)
    rank = jnp.full((nk, n), float(PEER_TOPK), F32)
    work = s
    for r in range(PEER_TOPK):
        mx = jnp.max(work, 0, keepdims=True)
        hit = work == mx
        if exact:
            hit = idx == jnp.min(jnp.where(hit, idx, float(nk)), 0, keepdims=True)
        rank = jnp.where(hit, float(r), rank)
        vals = jnp.where(row16 == r, mx, vals)
        work = jnp.where(hit, -jnp.inf, work)
    taken = jnp.sum(jnp.where(rank < float(PEER_TOPK), 1.0, 0.0), 0, keepdims=True)
    return vals, rank, taken


def _staircase(v1, v2, exact):
    n = v1.shape[1]
    k = PEER_TOPK
    half = k // 2
    slabs, cis = [], []
    r16 = lax.broadcasted_iota(jnp.int32, (k, n), 0)
    r8 = lax.broadcasted_iota(jnp.int32, (half, n), 0)
    slabs.append(v1 + v2[0:1])
    cis.append(r16 * k)
    for b in range(1, half):
        slabs.append(v1[0:half] + v2[b:b + 1])
        cis.append(r8 * k + b)
    slabs.append(v1[0:1] + v2[half:k])
    cis.append(r8 + half)
    cand = jnp.concatenate(slabs, 0)
    ci = jnp.concatenate(cis, 0).astype(F32)
    big = float(k * k)
    work = cand
    sel = jnp.zeros(cand.shape, jnp.bool_)
    for _ in range(k):
        mx = jnp.max(work, 0, keepdims=True)
        hit = work == mx
        if exact:
            hit = ci == jnp.min(jnp.where(hit, ci, big), 0, keepdims=True)
        sel = jnp.logical_or(sel, hit)
        work = jnp.where(hit, -jnp.inf, work)
    top = cand[0:1]
    z = jnp.sum(jnp.where(sel, jnp.exp(cand - top), 0.0), 0, keepdims=True)
    self = sel.astype(F32)
    lo = self[0:half]
    for b in range(1, half):
        lo = lo + self[k + (b - 1) * half:k + b * half]
    tail = jnp.sum(self[k + (half - 1) * half:], 0, keepdims=True)
    lo = lo + jnp.where(r8 == 0, tail, 0.0)
    counts = jnp.concatenate([lo, self[half:k]], 0)
    return counts, z, jnp.sum(self, 0, keepdims=True)


def _route_chunk(s1, s2, exact):
    v1, rank1, n1 = _top16(s1, exact)
    v2, rank2, n2 = _top16(s2, exact)
    counts, z, n3 = _staircase(v1, v2, exact)
    lr = jnp.zeros_like(rank1)
    for a in range(PEER_TOPK):
        lr = jnp.where(rank1 == float(a), counts[a:a + 1], lr)
    e1 = jnp.exp(s1 - v1[0:1]) / z
    e2 = jnp.exp(s2 - v2[0:1])
    untied = jnp.logical_and(jnp.logical_and(n1 == float(PEER_TOPK), n2 == float(PEER_TOPK)),
                             n3 == float(PEER_TOPK))
    return (lr, e1, rank2, e2), untied


def _peer_route_kernel(h_ref, mod_ref, g_ref, wqt_ref, keys_ref, hnt_ref, lr_ref, e1_ref, rb_ref, e2_ref, qt_ref):
    hd = pl.program_id(1)
    tm = h_ref.shape[0]

    @pl.when(hd == 0)
    def _():
        x = h_ref[...]
        hn = x * _rms_scale(x) * g_ref[...]
        hn = hn * (1.0 + mod_ref[0, 4:5, :]) + mod_ref[0, 3:4, :]
        hnt = hn.T.astype(BF16)
        hnt_ref[...] = hnt
        qt_ref[...] = jnp.dot(wqt_ref[...], hnt, preferred_element_type=F32).astype(BF16)

    q0 = pl.multiple_of(hd * PEER_QDIM, PEER_QDIM)
    k1 = keys_ref[2 * hd]
    k2 = keys_ref[2 * hd + 1]

    def route(exact):
        untied = None
        for c in range(tm // LANE):
            t0 = c * LANE
            s1 = jnp.dot(k1, qt_ref[pl.ds(q0, PEER_HALF), t0:t0 + LANE], preferred_element_type=F32)
            s2 = jnp.dot(k2, qt_ref[pl.ds(q0 + PEER_HALF, PEER_HALF), t0:t0 + LANE],
                         preferred_element_type=F32)
            values, ok = _route_chunk(s1, s2, exact)
            for ref, val in zip((lr_ref, e1_ref, rb_ref, e2_ref), values):
                ref[0, :, t0:t0 + LANE] = val.astype(ref.dtype)
            untied = ok if untied is None else jnp.logical_and(untied, ok)
        return untied

    untied = route(exact=False)

    @pl.when(jnp.logical_not(jnp.all(untied)))
    def _():
        route(exact=True)


def _peer_route(h, mod, mod_row_map, norm_g, wq_t, keys, name):
    m, d = h.shape
    tm = _tile(m, TOKEN_TILE)
    qd = wq_t.shape[0]
    gate_spec = pl.BlockSpec((1, PEER_KEYS, tm), lambda i, hd: (hd, 0, i))
    gate_shape = jax.ShapeDtypeStruct((PEER_HEADS, PEER_KEYS, m), F32)
    gate_shape_bf16 = jax.ShapeDtypeStruct((PEER_HEADS, PEER_KEYS, m), BF16)
    return pl.pallas_call(
        _peer_route_kernel,
        grid=(m // tm, PEER_HEADS),
        in_specs=[
            pl.BlockSpec((tm, d), lambda i, hd: (i, 0)),
            pl.BlockSpec((1, N_MOD, d), lambda i, hd: (mod_row_map(i), 0, 0)),
            pl.BlockSpec((1, d), lambda i, hd: (0, 0)),
            pl.BlockSpec((qd, d), lambda i, hd: (0, 0)),
            pl.BlockSpec((2 * PEER_HEADS, PEER_KEYS, PEER_HALF), lambda i, hd: (0, 0, 0)),
        ],
        out_specs=[pl.BlockSpec((d, tm), lambda i, hd: (0, i)), gate_spec, gate_spec, gate_spec, gate_spec],
        out_shape=[jax.ShapeDtypeStruct((d, m), BF16), gate_shape, gate_shape, gate_shape_bf16, gate_shape_bf16],
        scratch_shapes=[pltpu.VMEM((qd, tm), BF16)],
        compiler_params=_cparams(("arbitrary", "arbitrary")),
        name=name,
    )(h, mod, norm_g, wq_t, keys)


def _peer_dense_kernel(hnt_ref, lr_ref, e1_ref, rb_ref, e2_ref, u_ref, vt_ref, o_ref,
                       hid_even, hid_odd, a_ref, *, n_tiles):
    j = pl.program_id(1)
    te, tm = hid_even.shape
    per_step = te // PEER_KEYS

    def project(hid_ref):
        hid_ref[...] = jnp.dot(u_ref[...], hnt_ref[...], preferred_element_type=F32)

    def finish(hid_ref, jj):
        i1_base = pl.multiple_of(jj * per_step, SUBLANE)
        for c in range(tm // LANE):
            t0 = c * LANE
            lr_rows = [lr_ref[hd, pl.ds(i1_base, per_step), t0:t0 + LANE] for hd in range(PEER_HEADS)]
            e1_rows = [e1_ref[hd, pl.ds(i1_base, per_step), t0:t0 + LANE] for hd in range(PEER_HEADS)]
            for il in range(per_step):
                r0 = il * PEER_KEYS
                gate = jnp.zeros((PEER_KEYS, LANE), BF16)
                for hd in range(PEER_HEADS):
                    lr = jnp.broadcast_to(lr_rows[hd][il:il + 1], (PEER_KEYS, LANE)).astype(BF16)
                    e1 = jnp.broadcast_to(e1_rows[hd][il:il + 1], (PEER_KEYS, LANE)).astype(BF16)
                    step = jnp.clip(lr - rb_ref[hd, :, t0:t0 + LANE], 0.0, 1.0)
                    gate = gate + (step * e2_ref[hd, :, t0:t0 + LANE]) * e1
                act = jax.nn.gelu(hid_ref[r0:r0 + PEER_KEYS, t0:t0 + LANE].astype(BF16))
                a_ref[r0:r0 + PEER_KEYS, t0:t0 + LANE] = act * gate
        o_ref[...] += jnp.dot(vt_ref[0], a_ref[...], preferred_element_type=F32)

    @pl.when(j == 0)
    def _():
        o_ref[...] = jnp.zeros_like(o_ref)
        project(hid_even)

    middle = jnp.logical_and(j > 0, j < n_tiles)

    @pl.when(jnp.logical_and(middle, j % 2 == 1))
    def _():
        project(hid_odd)
        finish(hid_even, j - 1)

    @pl.when(jnp.logical_and(middle, j % 2 == 0))
    def _():
        project(hid_even)
        finish(hid_odd, j - 1)

    @pl.when(j == n_tiles)
    def _():
        finish(hid_odd if n_tiles % 2 == 0 else hid_even, n_tiles - 1)


def _peer_dense(hnt, lr, e1, rb, e2, u, vt_tiles, name):
    d, m = hnt.shape
    n_tiles, _, te = vt_tiles.shape
    tm = _tile(m, TOKEN_TILE)
    gate_spec = pl.BlockSpec((PEER_HEADS, PEER_KEYS, tm), lambda i, j: (0, 0, i))
    return pl.pallas_call(
        functools.partial(_peer_dense_kernel, n_tiles=n_tiles),
        grid=(m // tm, n_tiles + 1),
        in_specs=[
            pl.BlockSpec((d, tm), lambda i, j: (0, i)),
            gate_spec, gate_spec, gate_spec, gate_spec,
            pl.BlockSpec((te, d), lambda i, j: (jnp.minimum(j, n_tiles - 1), 0)),
            pl.BlockSpec((1, d, te), lambda i, j: (jnp.maximum(j - 1, 0), 0, 0)),
        ],
        out_specs=pl.BlockSpec((d, tm), lambda i, j: (0, i)),
        out_shape=jax.ShapeDtypeStruct((d, m), F32),
        scratch_shapes=[pltpu.VMEM((te, tm), F32), pltpu.VMEM((te, tm), F32), pltpu.VMEM((te, tm), BF16)],
        compiler_params=_cparams(("arbitrary", "arbitrary")),
        name=name,
    )(hnt, lr, e1, rb, e2, u, vt_tiles)


def _peer_residual_kernel(h_ref, mod_ref, ot_ref, o_ref):
    o_ref[...] = h_ref[...] + mod_ref[0, 5:6, :] * ot_ref[...].T


def _peer_residual(h, mod, mod_row_map, out_t, name):
    m, d = h.shape
    tm = _tile(m, TOKEN_TILE)
    return pl.pallas_call(
        _peer_residual_kernel,
        grid=(m // tm,),
        in_specs=[
            pl.BlockSpec((tm, d), lambda i: (i, 0)),
            pl.BlockSpec((1, N_MOD, d), lambda i: (mod_row_map(i), 0, 0)),
            pl.BlockSpec((d, tm), lambda i: (0, i)),
        ],
        out_specs=pl.BlockSpec((tm, d), lambda i: (i, 0)),
        out_shape=jax.ShapeDtypeStruct((m, d), F32),
        compiler_params=_cparams(("arbitrary",)),
        name=name,
    )(h, mod, out_t)


def _rope_tables(seq):
    t = jnp.arange(seq)
    row = (t // GRID_W).astype(F32)
    col = (t % GRID_W).astype(F32)
    inv = ROPE_THETA ** (-jnp.arange(AXIS_DIM // 2, dtype=F32) / (AXIS_DIM // 2))
    ar, ac = row[:, None] * inv, col[:, None] * inv
    ang = jnp.concatenate([ar, ar, ac, ac], -1)
    sign = jnp.tile(jnp.concatenate([-jnp.ones(AXIS_DIM // 2, F32), jnp.ones(AXIS_DIM // 2, F32)]), 2)
    return jnp.cos(ang), jnp.sin(ang) * sign


def _gain_table(qn_a, kn_a, qn_b, kn_b, vn_c):
    scale = HEAD_DIM ** -0.5
    ones_a = jnp.ones((A_KV_WIDTH,), F32)
    ones_b = jnp.ones((B_KV_WIDTH,), F32)
    return jnp.concatenate([
        jnp.tile(qn_a * scale, A_HEADS), jnp.tile(kn_a, A_KV), ones_a,
        jnp.tile(qn_b * scale, B_HEADS), jnp.tile(kn_b, B_KV), ones_b,
        jnp.ones((C_WIDTH,), F32), vn_c,
    ]).reshape(1, IN_WIDTH)


def kernel(x, c, ctx, c_ctx, w_ada, b_ada, norm1_g, norm2_g, w_in, qn_a, kn_a, qn_b, kn_b, sink_b, vn_c, ws_c,
           bs_c, g_group, w_out, peer_wq, peer_keys, peer_u, peer_v):
    batch, seq, d = x.shape
    n_ctx = ctx.shape[1]
    depth = w_ada.shape[0]
    assert batch < MOD_ROWS and seq % TOKEN_TILE == 0 and TOKEN_TILE % n_ctx == 0
    assert seq % GRID_W == 0 and n_ctx % CHUNK == 0

    ctx_row = batch
    cc = jnp.zeros((MOD_ROWS, d), F32).at[:batch].set(c).at[ctx_row].set(c_ctx)
    mod_all = _ada_modulation(cc, w_ada, b_ada).reshape(depth, MOD_ROWS, N_MOD, d)

    cos, sin_signed = _rope_tables(seq)
    tiles_per_seq = seq // TOKEN_TILE
    lat_row = lambda i: i // tiles_per_seq
    ctx_row_map = lambda i: ctx_row
    no_sink = jnp.full((A_HEADS,), NEG_INF, F32)

    h = x.reshape(batch * seq, d)
    hc = ctx.reshape(batch * n_ctx, d)
    for l in range(depth):
        last = l == depth - 1
        mod = mod_all[l]
        w_in_l = w_in[l].astype(BF16)
        w_out_l = w_out[l].astype(BF16)
        gains = _gain_table(qn_a[l], kn_a[l], qn_b[l], kn_b[l], vn_c[l])
        n1 = norm1_g[l].reshape(1, d)
        n2 = norm2_g[l].reshape(1, d)
        ws = ws_c[l].astype(BF16)
        bs_b = jnp.broadcast_to(bs_c[l][:, :, None], (C_GROUPS, CHUNK, C_GROUP_DIM))
        gg = g_group[l].reshape(1, MIX_WIDTH)
        wq_t = peer_wq[l].T.astype(BF16)
        keys = peer_keys[l].reshape(2 * PEER_HEADS, PEER_KEYS, PEER_HALF).astype(BF16)
        u_tab = peer_u[l].astype(BF16)
        n_exp = peer_v.shape[1]
        te = _tile(n_exp, EXPERT_TILE)
        v_tiles = peer_v[l].astype(BF16).reshape(n_exp // te, te, d).transpose(0, 2, 1)

        proj = _in_projection(h, mod, lat_row, n1, w_in_l, gains, cos, sin_signed,
                              tiles_per_seq=tiles_per_seq, use_rope=True)
        proj_c = _in_projection(hc, mod, ctx_row_map, n1, w_in_l, gains, cos, sin_signed,
                                tiles_per_seq=tiles_per_seq, use_rope=False)
        ya = _global_attention(proj, proj_c, batch, seq, n_ctx)
        yb = _window_attention(sink_b[l], proj, proj_c, batch, seq, n_ctx)
        h = _out_projection(h, mod, lat_row, ya, yb, proj, ws, bs_b, gg, w_out_l, "out_projection")
        routed = _peer_route(h, mod, lat_row, n2, wq_t, keys, "peer_route")
        h = _peer_residual(h, mod, lat_row, _peer_dense(*routed, u_tab, v_tiles, "peer_dense"), "peer_residual")
        if not last:
            cya = _context_attention(no_sink, proj_c, batch, n_ctx, OFF_QA, OFF_KA, OFF_VA, "attn_ctx_a")
            cyb = _context_attention(sink_b[l], proj_c, batch, n_ctx, OFF_QB, OFF_KB, OFF_VB, "attn_ctx_b")
            hc = _out_projection(hc, mod, ctx_row_map, cya, cyb, proj_c, ws, bs_b, gg, w_out_l, "out_projection_ctx")
            routed_c = _peer_route(hc, mod, ctx_row_map, n2, wq_t, keys, "peer_route_ctx")
            hc = _peer_residual(hc, mod, ctx_row_map, _peer_dense(*routed_c, u_tab, v_tiles, "peer_dense_ctx"),
                                "peer_residual_ctx")
    return h.reshape(batch, seq, d)
```

```python
import functools

import jax
import jax.numpy as jnp
from jax import lax
from jax.experimental import pallas as pl
from jax.experimental.pallas import tpu as pltpu

F32 = jnp.float32
BF16 = jnp.bfloat16

GRID_W = 64
HEAD_DIM = 128
AXIS_DIM = HEAD_DIM // 2
ROPE_THETA = 10000.0
A_HEADS, A_KV = 6, 2
B_HEADS, B_KV = 6, 2
WINDOW = 128
C_GROUPS, C_GROUP_DIM, CHUNK = 4, 128, 128
A_WIDTH = A_HEADS * HEAD_DIM
B_WIDTH = B_HEADS * HEAD_DIM
C_WIDTH = C_GROUPS * C_GROUP_DIM
MIX_WIDTH = A_WIDTH + B_WIDTH + C_WIDTH
A_KV_WIDTH = A_KV * HEAD_DIM
B_KV_WIDTH = B_KV * HEAD_DIM
IN_WIDTH = A_WIDTH + 2 * A_KV_WIDTH + B_WIDTH + 2 * B_KV_WIDTH + 2 * C_WIDTH
GROUP = A_HEADS // A_KV
PEER_HEADS = 8
PEER_KEYS = 128
PEER_QDIM = 256
PEER_HALF = PEER_QDIM // 2
PEER_TOPK = 16
N_MOD = 6
EPS = 1e-6
NEG_INF = -1e30

OFF_QA = 0
OFF_KA = OFF_QA + A_WIDTH
OFF_VA = OFF_KA + A_KV_WIDTH
OFF_QB = OFF_VA + A_KV_WIDTH
OFF_KB = OFF_QB + B_WIDTH
OFF_VB = OFF_KB + B_KV_WIDTH
OFF_UC = OFF_VB + B_KV_WIDTH
OFF_VC = OFF_UC + C_WIDTH

MOD_ROWS = 16
TOKEN_TILE = 512
IN_COL_TILE = 256
OUT_COL_TILE = 512
ADA_COL_TILE = 1024
Q_TILE_GLOBAL = 256
Q_TILE_WINDOW = 128
EXPERT_TILE = 1024
LANE = 128
SUBLANE = 8
VMEM_LIMIT = 56 * 1024 * 1024


def _tile(n, preferred):
    t = min(preferred, n)
    while n % t:
        t -= LANE
    return t


def _cparams(sem):
    return pltpu.CompilerParams(dimension_semantics=sem, vmem_limit_bytes=VMEM_LIMIT)


def _rms_scale(x):
    return lax.rsqrt(jnp.mean(x * x, -1, keepdims=True) + EPS)


def _ada_kernel(c_ref, w_ref, b_ref, o_ref):
    c = c_ref[...]
    a = (c * jax.nn.sigmoid(c)).astype(BF16)
    o_ref[0] = jnp.dot(a, w_ref[0].astype(BF16), preferred_element_type=F32) + b_ref[0]


def _ada_modulation(cc, w_ada, b_ada):
    depth, d, n = w_ada.shape
    tn = _tile(n, ADA_COL_TILE)
    return pl.pallas_call(
        _ada_kernel,
        grid=(depth, n // tn),
        in_specs=[
            pl.BlockSpec((MOD_ROWS, d), lambda l, j: (0, 0)),
            pl.BlockSpec((1, d, tn), lambda l, j: (l, 0, j)),
            pl.BlockSpec((1, 1, tn), lambda l, j: (l, 0, j)),
        ],
        out_specs=pl.BlockSpec((1, MOD_ROWS, tn), lambda l, j: (l, 0, j)),
        out_shape=jax.ShapeDtypeStruct((depth, MOD_ROWS, n), F32),
        compiler_params=_cparams(("arbitrary", "arbitrary")),
        name="ada_modulation",
    )(cc, w_ada, b_ada.reshape(depth, 1, n))


def _col_tile_kinds():
    kinds = []
    for j in range(IN_WIDTH // IN_COL_TILE):
        c = j * IN_COL_TILE
        if c < OFF_VA or OFF_QB <= c < OFF_VB:
            kinds.append("head")
        elif c < OFF_QB or c < OFF_UC:
            kinds.append("plain")
        elif c < OFF_VC:
            kinds.append("gelu")
        else:
            kinds.append("gelu_norm")
    return kinds


def _inproj_kernel(h_ref, mod_ref, g_ref, w_ref, gain_ref, cos_ref, sin_ref, o_ref, *, use_rope):
    x = h_ref[...]
    xn = x * _rms_scale(x) * g_ref[...]
    xn = (xn * (1.0 + mod_ref[0, 1:2, :]) + mod_ref[0, 0:1, :]).astype(BF16)

    def rope(a):
        if not use_rope:
            return a
        lane = lax.broadcasted_iota(jnp.int32, a.shape, 1)
        first = (lane // (AXIS_DIM // 2)) % 2 == 0
        rot = jnp.where(first, pltpu.roll(a, HEAD_DIM - AXIS_DIM // 2, 1), pltpu.roll(a, AXIS_DIM // 2, 1))
        return a * cos_ref[...] + rot * sin_ref[...]

    def gelu_norm(a, g):
        a = jax.nn.gelu(a)
        return a * _rms_scale(a) * g

    epilogue = {
        "head": lambda a, g: rope(a * _rms_scale(a) * g),
        "plain": lambda a, g: a,
        "gelu": lambda a, g: jax.nn.gelu(a),
        "gelu_norm": gelu_norm,
    }
    for j, kind in enumerate(_col_tile_kinds()):
        c0 = j * IN_COL_TILE
        acc = jnp.dot(xn, w_ref[:, c0:c0 + IN_COL_TILE], preferred_element_type=F32)
        for k in range(IN_COL_TILE // LANE):
            lo = c0 + k * LANE
            part = epilogue[kind](acc[:, k * LANE:(k + 1) * LANE], gain_ref[:, lo:lo + LANE])
            o_ref[:, lo:lo + LANE] = part.astype(o_ref.dtype)


def _in_projection(h, mod, mod_row_map, norm_g, w_in, gains, cos, sin_signed, *, tiles_per_seq, use_rope):
    m, d = h.shape
    tm = _tile(m, TOKEN_TILE)
    if use_rope:
        pos_map = lambda i: (i % tiles_per_seq, 0)
    else:
        pos_map = lambda i: (0, 0)
    return pl.pallas_call(
        functools.partial(_inproj_kernel, use_rope=use_rope),
        grid=(m // tm,),
        in_specs=[
            pl.BlockSpec((tm, d), lambda i: (i, 0)),
            pl.BlockSpec((1, N_MOD, d), lambda i: (mod_row_map(i), 0, 0)),
            pl.BlockSpec((1, d), lambda i: (0, 0)),
            pl.BlockSpec((d, IN_WIDTH), lambda i: (0, 0)),
            pl.BlockSpec((1, IN_WIDTH), lambda i: (0, 0)),
            pl.BlockSpec((tm, HEAD_DIM), pos_map),
            pl.BlockSpec((tm, HEAD_DIM), pos_map),
        ],
        out_specs=pl.BlockSpec((tm, IN_WIDTH), lambda i: (i, 0)),
        out_shape=jax.ShapeDtypeStruct((m, IN_WIDTH), BF16),
        compiler_params=_cparams(("arbitrary",)),
        name="in_projection_rope" if use_rope else "in_projection_ctx",
    )(h, mod, norm_g, w_in, gains, cos, sin_signed)


_NT = (((1,), (1,)), ((), ()))


def _stack_heads(q_refs):
    return jnp.concatenate([r[...] for r in q_refs], 0)


def _q_specs(rows, row_fn, off_q):
    def spec(g):
        return pl.BlockSpec((rows, HEAD_DIM),
                            lambda *ids: (row_fn(*ids), off_q // HEAD_DIM + ids[1] * GROUP + g))
    return [spec(g) for g in range(GROUP)]


def _unstack_heads(o, tq):
    return jnp.concatenate([o[g * tq:(g + 1) * tq] for g in range(GROUP)], 1)


def _sink_column(sink_ref, kv, tq):
    row = lax.broadcasted_iota(jnp.int32, (GROUP * tq, 1), 0)
    col = jnp.full((GROUP * tq, 1), sink_ref[kv * GROUP + GROUP - 1], F32)
    for g in range(GROUP - 2, -1, -1):
        col = jnp.where(row < (g + 1) * tq, sink_ref[kv * GROUP + g], col)
    return col


def _attn_global_kernel(q0_ref, q1_ref, q2_ref, kc_ref, vc_ref, kl_ref, vl_ref, o_ref):
    tq = q0_ref.shape[0]
    qs = _stack_heads((q0_ref, q1_ref, q2_ref))
    sc = lax.dot_general(qs, kc_ref[...], _NT, preferred_element_type=F32)
    sl = lax.dot_general(qs, kl_ref[...], _NT, preferred_element_type=F32)
    m = jnp.maximum(jnp.max(sc, -1, keepdims=True), jnp.max(sl, -1, keepdims=True))
    pc = jnp.exp(sc - m)
    pw = jnp.exp(sl - m)
    denom = jnp.sum(pc, -1, keepdims=True) + jnp.sum(pw, -1, keepdims=True)
    o = (jnp.dot(pc.astype(BF16), vc_ref[...], preferred_element_type=F32)
         + jnp.dot(pw.astype(BF16), vl_ref[...], preferred_element_type=F32))
    o_ref[...] = _unstack_heads(o / denom, tq).astype(o_ref.dtype)


def _global_attention(proj, proj_ctx, batch, seq, n_ctx):
    tq = _tile(seq, Q_TILE_GLOBAL)
    nq = seq // tq
    qw = GROUP * HEAD_DIM
    return pl.pallas_call(
        _attn_global_kernel,
        grid=(batch, A_KV, nq),
        in_specs=[
            *_q_specs(tq, lambda b, k, i: b * nq + i, OFF_QA),
            pl.BlockSpec((n_ctx, HEAD_DIM), lambda b, k, i: (b, OFF_KA // HEAD_DIM + k)),
            pl.BlockSpec((n_ctx, HEAD_DIM), lambda b, k, i: (b, OFF_VA // HEAD_DIM + k)),
            pl.BlockSpec((seq, HEAD_DIM), lambda b, k, i: (b, OFF_KA // HEAD_DIM + k)),
            pl.BlockSpec((seq, HEAD_DIM), lambda b, k, i: (b, OFF_VA // HEAD_DIM + k)),
        ],
        out_specs=pl.BlockSpec((tq, qw), lambda b, k, i: (b * nq + i, k)),
        out_shape=jax.ShapeDtypeStruct((batch * seq, A_WIDTH), BF16),
        compiler_params=_cparams(("arbitrary", "arbitrary", "arbitrary")),
        name="attn_global",
    )(proj, proj, proj, proj_ctx, proj_ctx, proj, proj)


def _attn_window_kernel(sink_ref, q0_ref, q1_ref, q2_ref, kc_ref, vc_ref, kl_ref, vl_ref, o_ref):
    tq = q0_ref.shape[0]
    seq = kl_ref.shape[0]
    span = tq + 2 * WINDOW
    kv = pl.program_id(1)
    i = pl.program_id(2)
    start = pl.multiple_of(jnp.clip(i * tq - WINDOW, 0, seq - span), LANE)
    qs = _stack_heads((q0_ref, q1_ref, q2_ref))
    sw = lax.dot_general(qs, kl_ref[pl.ds(start, span), :], _NT, preferred_element_type=F32)
    qpos = i * tq + lax.broadcasted_iota(jnp.int32, (tq, span), 0)
    kpos = start + lax.broadcasted_iota(jnp.int32, (tq, span), 1)
    valid = jnp.abs(kpos - qpos) <= WINDOW
    sw = jnp.where(jnp.concatenate([valid] * GROUP, 0), sw, NEG_INF)
    sc = lax.dot_general(qs, kc_ref[...], _NT, preferred_element_type=F32)
    ss = _sink_column(sink_ref, kv, tq)
    m = jnp.maximum(jnp.maximum(jnp.max(sw, -1, keepdims=True), jnp.max(sc, -1, keepdims=True)), ss)
    pw = jnp.exp(sw - m)
    pc = jnp.exp(sc - m)
    denom = jnp.sum(pw, -1, keepdims=True) + jnp.sum(pc, -1, keepdims=True) + jnp.exp(ss - m)
    o = (jnp.dot(pw.astype(BF16), vl_ref[pl.ds(start, span), :], preferred_element_type=F32)
         + jnp.dot(pc.astype(BF16), vc_ref[...], preferred_element_type=F32))
    o_ref[...] = _unstack_heads(o / denom, tq).astype(o_ref.dtype)


def _window_attention(sink, proj, proj_ctx, batch, seq, n_ctx):
    tq = Q_TILE_WINDOW
    nq = seq // tq
    qw = GROUP * HEAD_DIM
    return pl.pallas_call(
        _attn_window_kernel,
        grid=(batch, B_KV, nq),
        in_specs=[
            pl.BlockSpec(memory_space=pltpu.SMEM),
            *_q_specs(tq, lambda b, k, i: b * nq + i, OFF_QB),
            pl.BlockSpec((n_ctx, HEAD_DIM), lambda b, k, i: (b, OFF_KB // HEAD_DIM + k)),
            pl.BlockSpec((n_ctx, HEAD_DIM), lambda b, k, i: (b, OFF_VB // HEAD_DIM + k)),
            pl.BlockSpec((seq, HEAD_DIM), lambda b, k, i: (b, OFF_KB // HEAD_DIM + k)),
            pl.BlockSpec((seq, HEAD_DIM), lambda b, k, i: (b, OFF_VB // HEAD_DIM + k)),
        ],
        out_specs=pl.BlockSpec((tq, qw), lambda b, k, i: (b * nq + i, k)),
        out_shape=jax.ShapeDtypeStruct((batch * seq, B_WIDTH), BF16),
        compiler_params=_cparams(("arbitrary", "arbitrary", "arbitrary")),
        name="attn_window",
    )(sink, proj, proj, proj, proj_ctx, proj_ctx, proj, proj)


def _attn_ctx_kernel(sink_ref, q0_ref, q1_ref, q2_ref, k_ref, v_ref, o_ref):
    tq = q0_ref.shape[0]
    kv = pl.program_id(1)
    qs = _stack_heads((q0_ref, q1_ref, q2_ref))
    s = lax.dot_general(qs, k_ref[...], _NT, preferred_element_type=F32)
    ss = _sink_column(sink_ref, kv, tq)
    m = jnp.maximum(jnp.max(s, -1, keepdims=True), ss)
    p = jnp.exp(s - m)
    denom = jnp.sum(p, -1, keepdims=True) + jnp.exp(ss - m)
    o = jnp.dot(p.astype(BF16), v_ref[...], preferred_element_type=F32)
    o_ref[...] = _unstack_heads(o / denom, tq).astype(o_ref.dtype)


def _context_attention(sink, proj_ctx, batch, n_ctx, off_q, off_k, off_v, name):
    qw = GROUP * HEAD_DIM
    n_kv = A_KV
    return pl.pallas_call(
        _attn_ctx_kernel,
        grid=(batch, n_kv),
        in_specs=[
            pl.BlockSpec(memory_space=pltpu.SMEM),
            *_q_specs(n_ctx, lambda b, k: b, off_q),
            pl.BlockSpec((n_ctx, HEAD_DIM), lambda b, k: (b, off_k // HEAD_DIM + k)),
            pl.BlockSpec((n_ctx, HEAD_DIM), lambda b, k: (b, off_v // HEAD_DIM + k)),
        ],
        out_specs=pl.BlockSpec((n_ctx, qw), lambda b, k: (b, k)),
        out_shape=jax.ShapeDtypeStruct((batch * n_ctx, n_kv * qw), BF16),
        compiler_params=_cparams(("arbitrary", "arbitrary")),
        name=name,
    )(sink, proj_ctx, proj_ctx, proj_ctx, proj_ctx, proj_ctx)


def _outproj_kernel(h_ref, mod_ref, ya_ref, yb_ref, u_ref, v_ref, ws_ref, bs_ref, gg_ref, w_ref, o_ref, *, tn):
    tm, d = h_ref.shape
    u = u_ref[...]
    v = v_ref[...]
    rows = []
    for c in range(tm // CHUNK):
        r0 = c * CHUNK
        cols = []
        for g in range(C_GROUPS):
            c0 = g * C_GROUP_DIM
            s = jnp.dot(ws_ref[g], v[r0:r0 + CHUNK, c0:c0 + C_GROUP_DIM],
                        preferred_element_type=F32) + bs_ref[g]
            cols.append(u[r0:r0 + CHUNK, c0:c0 + C_GROUP_DIM].astype(F32) * s)
        rows.append(jnp.concatenate(cols, 1))
    yc = jnp.concatenate(rows, 0)
    ya = ya_ref[...].astype(F32)
    yb = yb_ref[...].astype(F32)
    y = jnp.concatenate([ya * _rms_scale(ya), yb * _rms_scale(yb), yc * _rms_scale(yc)], 1) * gg_ref[...]
    y = y.astype(BF16)
    for c0 in range(0, d, tn):
        acc = jnp.dot(y, w_ref[:, c0:c0 + tn], preferred_element_type=F32)
        o_ref[:, c0:c0 + tn] = h_ref[:, c0:c0 + tn] + mod_ref[0, 2:3, c0:c0 + tn] * acc


def _out_projection(h, mod, mod_row_map, ya, yb, proj, ws, bs_b, g_group, w_out, name):
    m, d = h.shape
    tm = _tile(m, TOKEN_TILE)
    return pl.pallas_call(
        functools.partial(_outproj_kernel, tn=_tile(d, OUT_COL_TILE)),
        grid=(m // tm,),
        in_specs=[
            pl.BlockSpec((tm, d), lambda i: (i, 0)),
            pl.BlockSpec((1, N_MOD, d), lambda i: (mod_row_map(i), 0, 0)),
            pl.BlockSpec((tm, A_WIDTH), lambda i: (i, 0)),
            pl.BlockSpec((tm, B_WIDTH), lambda i: (i, 0)),
            pl.BlockSpec((tm, C_WIDTH), lambda i: (i, OFF_UC // C_WIDTH)),
            pl.BlockSpec((tm, C_WIDTH), lambda i: (i, OFF_VC // C_WIDTH)),
            pl.BlockSpec((C_GROUPS, CHUNK, CHUNK), lambda i: (0, 0, 0)),
            pl.BlockSpec((C_GROUPS, CHUNK, C_GROUP_DIM), lambda i: (0, 0, 0)),
            pl.BlockSpec((1, MIX_WIDTH), lambda i: (0, 0)),
            pl.BlockSpec((MIX_WIDTH, d), lambda i: (0, 0)),
        ],
        out_specs=pl.BlockSpec((tm, d), lambda i: (i, 0)),
        out_shape=jax.ShapeDtypeStruct((m, d), F32),
        compiler_params=_cparams(("arbitrary",)),
        name=name,
    )(h, mod, ya, yb, proj, proj, ws, bs_b, g_group, w_out)


def _top16(s, exact):
    nk, n = s.shape
    idx = lax.broadcasted_iota(jnp.int32, (nk, n), 0).astype(F32)
    row16 = lax.broadcasted_iota(jnp.int32, (PEER_TOPK, n), 0)
    vals = jnp.zeros((PEER_TOPK, n), F32)
    rank = jnp.full((nk, n), float(PEER_TOPK), F32)
    work = s
    for r in range(PEER_TOPK):
        mx = jnp.max(work, 0, keepdims=True)
        hit = work == mx
        if exact:
            hit = idx == jnp.min(jnp.where(hit, idx, float(nk)), 0, keepdims=True)
        rank = jnp.where(hit, float(r), rank)
        vals = jnp.where(row16 == r, mx, vals)
        work = jnp.where(hit, -jnp.inf, work)
    taken = jnp.sum(jnp.where(rank < float(PEER_TOPK), 1.0, 0.0), 0, keepdims=True)
    return vals, rank, taken


def _staircase(v1, v2, exact):
    n = v1.shape[1]
    k = PEER_TOPK
    half = k // 2
    slabs, cis = [], []
    r16 = lax.broadcasted_iota(jnp.int32, (k, n), 0)
    r8 = lax.broadcasted_iota(jnp.int32, (half, n), 0)
    slabs.append(v1 + v2[0:1])
    cis.append(r16 * k)
    for b in range(1, half):
        slabs.append(v1[0:half] + v2[b:b + 1])
        cis.append(r8 * k + b)
    slabs.append(v1[0:1] + v2[half:k])
    cis.append(r8 + half)
    cand = jnp.concatenate(slabs, 0)
    ci = jnp.concatenate(cis, 0).astype(F32)
    big = float(k * k)
    work = cand
    sel = jnp.zeros(cand.shape, jnp.bool_)
    for _ in range(k):
        mx = jnp.max(work, 0, keepdims=True)
        hit = work == mx
        if exact:
            hit = ci == jnp.min(jnp.where(hit, ci, big), 0, keepdims=True)
        sel = jnp.logical_or(sel, hit)
        work = jnp.where(hit, -jnp.inf, work)
    top = cand[0:1]
    z = jnp.sum(jnp.where(sel, jnp.exp(cand - top), 0.0), 0, keepdims=True)
    self = sel.astype(F32)
    lo = self[0:half]
    for b in range(1, half):
        lo = lo + self[k + (b - 1) * half:k + b * half]
    tail = jnp.sum(self[k + (half - 1) * half:], 0, keepdims=True)
    lo = lo + jnp.where(r8 == 0, tail, 0.0)
    counts = jnp.concatenate([lo, self[half:k]], 0)
    return counts, z, jnp.sum(self, 0, keepdims=True)


def _route_chunk(s1, s2, exact):
    v1, rank1, n1 = _top16(s1, exact)
    v2, rank2, n2 = _top16(s2, exact)
    counts, z, n3 = _staircase(v1, v2, exact)
    lr = jnp.zeros_like(rank1)
    for a in range(PEER_TOPK):
        lr = jnp.where(rank1 == float(a), counts[a:a + 1], lr)
    e1 = jnp.exp(s1 - v1[0:1]) / z
    e2 = jnp.exp(s2 - v2[0:1])
    untied = jnp.logical_and(jnp.logical_and(n1 == float(PEER_TOPK), n2 == float(PEER_TOPK)),
                             n3 == float(PEER_TOPK))
    return (lr, e1, rank2, e2), untied


def _peer_route_kernel(h_ref, mod_ref, g_ref, wqt_ref, keys_ref, hnt_ref, lr_ref, e1_ref, rb_ref, e2_ref, qt_ref):
    hd = pl.program_id(1)
    tm = h_ref.shape[0]

    @pl.when(hd == 0)
    def _():
        x = h_ref[...]
        hn = x * _rms_scale(x) * g_ref[...]
        hn = hn * (1.0 + mod_ref[0, 4:5, :]) + mod_ref[0, 3:4, :]
        hnt = hn.T.astype(BF16)
        hnt_ref[...] = hnt
        qt_ref[...] = jnp.dot(wqt_ref[...], hnt, preferred_element_type=F32).astype(BF16)

    q0 = pl.multiple_of(hd * PEER_QDIM, PEER_QDIM)
    k1 = keys_ref[2 * hd]
    k2 = keys_ref[2 * hd + 1]

    def route(exact):
        untied = None
        for c in range(tm // LANE):
            t0 = c * LANE
            s1 = jnp.dot(k1, qt_ref[pl.ds(q0, PEER_HALF), t0:t0 + LANE], preferred_element_type=F32)
            s2 = jnp.dot(k2, qt_ref[pl.ds(q0 + PEER_HALF, PEER_HALF), t0:t0 + LANE],
                         preferred_element_type=F32)
            values, ok = _route_chunk(s1, s2, exact)
            for ref, val in zip((lr_ref, e1_ref, rb_ref, e2_ref), values):
                ref[0, :, t0:t0 + LANE] = val.astype(ref.dtype)
            untied = ok if untied is None else jnp.logical_and(untied, ok)
        return untied

    untied = route(exact=False)

    @pl.when(jnp.logical_not(jnp.all(untied)))
    def _():
        route(exact=True)


def _peer_route(h, mod, mod_row_map, norm_g, wq_t, keys, name):
    m, d = h.shape
    tm = _tile(m, TOKEN_TILE)
    qd = wq_t.shape[0]
    gate_spec = pl.BlockSpec((1, PEER_KEYS, tm), lambda i, hd: (hd, 0, i))
    gate_shape = jax.ShapeDtypeStruct((PEER_HEADS, PEER_KEYS, m), F32)
    gate_shape_bf16 = jax.ShapeDtypeStruct((PEER_HEADS, PEER_KEYS, m), BF16)
    return pl.pallas_call(
        _peer_route_kernel,
        grid=(m // tm, PEER_HEADS),
        in_specs=[
            pl.BlockSpec((tm, d), lambda i, hd: (i, 0)),
            pl.BlockSpec((1, N_MOD, d), lambda i, hd: (mod_row_map(i), 0, 0)),
            pl.BlockSpec((1, d), lambda i, hd: (0, 0)),
            pl.BlockSpec((qd, d), lambda i, hd: (0, 0)),
            pl.BlockSpec((2 * PEER_HEADS, PEER_KEYS, PEER_HALF), lambda i, hd: (0, 0, 0)),
        ],
        out_specs=[pl.BlockSpec((d, tm), lambda i, hd: (0, i)), gate_spec, gate_spec, gate_spec, gate_spec],
        out_shape=[jax.ShapeDtypeStruct((d, m), BF16), gate_shape, gate_shape, gate_shape_bf16, gate_shape_bf16],
        scratch_shapes=[pltpu.VMEM((qd, tm), BF16)],
        compiler_params=_cparams(("arbitrary", "arbitrary")),
        name=name,
    )(h, mod, norm_g, wq_t, keys)


def _peer_dense_kernel(hnt_ref, lr_ref, e1_ref, rb_ref, e2_ref, u_ref, vt_ref, o_ref,
                       hid_even, hid_odd, a_ref, *, n_tiles):
    j = pl.program_id(1)
    te, tm = hid_even.shape
    per_step = te // PEER_KEYS

    def project(hid_ref):
        hid_ref[...] = jnp.dot(u_ref[...], hnt_ref[...], preferred_element_type=F32)

    def finish(hid_ref, jj):
        i1_base = pl.multiple_of(jj * per_step, SUBLANE)
        for c in range(tm // LANE):
            t0 = c * LANE
            lr_rows = [lr_ref[hd, pl.ds(i1_base, per_step), t0:t0 + LANE] for hd in range(PEER_HEADS)]
            e1_rows = [e1_ref[hd, pl.ds(i1_base, per_step), t0:t0 + LANE] for hd in range(PEER_HEADS)]
            for il in range(per_step):
                r0 = il * PEER_KEYS
                gate = jnp.zeros((PEER_KEYS, LANE), BF16)
                for hd in range(PEER_HEADS):
                    lr = jnp.broadcast_to(lr_rows[hd][il:il + 1], (PEER_KEYS, LANE)).astype(BF16)
                    e1 = jnp.broadcast_to(e1_rows[hd][il:il + 1], (PEER_KEYS, LANE)).astype(BF16)
                    step = jnp.clip(lr - rb_ref[hd, :, t0:t0 + LANE], 0.0, 1.0)
                    gate = gate + (step * e2_ref[hd, :, t0:t0 + LANE]) * e1
                act = jax.nn.gelu(hid_ref[r0:r0 + PEER_KEYS, t0:t0 + LANE].astype(BF16))
                a_ref[r0:r0 + PEER_KEYS, t0:t0 + LANE] = act * gate
        o_ref[...] += jnp.dot(vt_ref[0], a_ref[...], preferred_element_type=F32)

    @pl.when(j == 0)
    def _():
        o_ref[...] = jnp.zeros_like(o_ref)
        project(hid_even)

    middle = jnp.logical_and(j > 0, j < n_tiles)

    @pl.when(jnp.logical_and(middle, j % 2 == 1))
    def _():
        project(hid_odd)
        finish(hid_even, j - 1)

    @pl.when(jnp.logical_and(middle, j % 2 == 0))
    def _():
        project(hid_even)
        finish(hid_odd, j - 1)

    @pl.when(j == n_tiles)
    def _():
        finish(hid_odd if n_tiles % 2 == 0 else hid_even, n_tiles - 1)


def _peer_dense(hnt, lr, e1, rb, e2, u, vt_tiles, name):
    d, m = hnt.shape
    n_tiles, _, te = vt_tiles.shape
    tm = _tile(m, TOKEN_TILE)
    gate_spec = pl.BlockSpec((PEER_HEADS, PEER_KEYS, tm), lambda i, j: (0, 0, i))
    return pl.pallas_call(
        functools.partial(_peer_dense_kernel, n_tiles=n_tiles),
        grid=(m // tm, n_tiles + 1),
        in_specs=[
            pl.BlockSpec((d, tm), lambda i, j: (0, i)),
            gate_spec, gate_spec, gate_spec, gate_spec,
            pl.BlockSpec((te, d), lambda i, j: (jnp.minimum(j, n_tiles - 1), 0)),
            pl.BlockSpec((1, d, te), lambda i, j: (jnp.maximum(j - 1, 0), 0, 0)),
        ],
        out_specs=pl.BlockSpec((d, tm), lambda i, j: (0, i)),
        out_shape=jax.ShapeDtypeStruct((d, m), F32),
        scratch_shapes=[pltpu.VMEM((te, tm), F32), pltpu.VMEM((te, tm), F32), pltpu.VMEM((te, tm), BF16)],
        compiler_params=_cparams(("arbitrary", "arbitrary")),
        name=name,
    )(hnt, lr, e1, rb, e2, u, vt_tiles)


def _peer_residual_kernel(h_ref, mod_ref, ot_ref, o_ref):
    o_ref[...] = h_ref[...] + mod_ref[0, 5:6, :] * ot_ref[...].T


def _peer_residual(h, mod, mod_row_map, out_t, name):
    m, d = h.shape
    tm = _tile(m, TOKEN_TILE)
    return pl.pallas_call(
        _peer_residual_kernel,
        grid=(m // tm,),
        in_specs=[
            pl.BlockSpec((tm, d), lambda i: (i, 0)),
            pl.BlockSpec((1, N_MOD, d), lambda i: (mod_row_map(i), 0, 0)),
            pl.BlockSpec((d, tm), lambda i: (0, i)),
        ],
        out_specs=pl.BlockSpec((tm, d), lambda i: (i, 0)),
        out_shape=jax.ShapeDtypeStruct((m, d), F32),
        compiler_params=_cparams(("arbitrary",)),
        name=name,
    )(h, mod, out_t)


def _rope_tables(seq):
    t = jnp.arange(seq)
    row = (t // GRID_W).astype(F32)
    col = (t % GRID_W).astype(F32)
    inv = ROPE_THETA ** (-jnp.arange(AXIS_DIM // 2, dtype=F32) / (AXIS_DIM // 2))
    ar, ac = row[:, None] * inv, col[:, None] * inv
    ang = jnp.concatenate([ar, ar, ac, ac], -1)
    sign = jnp.tile(jnp.concatenate([-jnp.ones(AXIS_DIM // 2, F32), jnp.ones(AXIS_DIM // 2, F32)]), 2)
    return jnp.cos(ang), jnp.sin(ang) * sign


def _gain_table(qn_a, kn_a, qn_b, kn_b, vn_c):
    scale = HEAD_DIM ** -0.5
    ones_a = jnp.ones((A_KV_WIDTH,), F32)
    ones_b = jnp.ones((B_KV_WIDTH,), F32)
    return jnp.concatenate([
        jnp.tile(qn_a * scale, A_HEADS), jnp.tile(kn_a, A_KV), ones_a,
        jnp.tile(qn_b * scale, B_HEADS), jnp.tile(kn_b, B_KV), ones_b,
        jnp.ones((C_WIDTH,), F32), vn_c,
    ]).reshape(1, IN_WIDTH)


def kernel(x, c, ctx, c_ctx, w_ada, b_ada, norm1_g, norm2_g, w_in, qn_a, kn_a, qn_b, kn_b, sink_b, vn_c, ws_c,
           bs_c, g_group, w_out, peer_wq, peer_keys, peer_u, peer_v):
    batch, seq, d = x.shape
    n_ctx = ctx.shape[1]
    depth = w_ada.shape[0]
    assert batch < MOD_ROWS and seq % TOKEN_TILE == 0 and TOKEN_TILE % n_ctx == 0
    assert seq % GRID_W == 0 and n_ctx % CHUNK == 0

    ctx_row = batch
    cc = jnp.zeros((MOD_ROWS, d), F32).at[:batch].set(c).at[ctx_row].set(c_ctx)
    mod_all = _ada_modulation(cc, w_ada, b_ada).reshape(depth, MOD_ROWS, N_MOD, d)

    cos, sin_signed = _rope_tables(seq)
    tiles_per_seq = seq // TOKEN_TILE
    lat_row = lambda i: i // tiles_per_seq
    ctx_row_map = lambda i: ctx_row
    no_sink = jnp.full((A_HEADS,), NEG_INF, F32)

    h = x.reshape(batch * seq, d)
    hc = ctx.reshape(batch * n_ctx, d)
    for l in range(depth):
        last = l == depth - 1
        mod = mod_all[l]
        w_in_l = w_in[l].astype(BF16)
        w_out_l = w_out[l].astype(BF16)
        gains = _gain_table(qn_a[l], kn_a[l], qn_b[l], kn_b[l], vn_c[l])
        n1 = norm1_g[l].reshape(1, d)
        n2 = norm2_g[l].reshape(1, d)
        ws = ws_c[l].astype(BF16)
        bs_b = jnp.broadcast_to(bs_c[l][:, :, None], (C_GROUPS, CHUNK, C_GROUP_DIM))
        gg = g_group[l].reshape(1, MIX_WIDTH)
        wq_t = peer_wq[l].T.astype(BF16)
        keys = peer_keys[l].reshape(2 * PEER_HEADS, PEER_KEYS, PEER_HALF).astype(BF16)
        u_tab = peer_u[l].astype(BF16)
        n_exp = peer_v.shape[1]
        te = _tile(n_exp, EXPERT_TILE)
        v_tiles = peer_v[l].astype(BF16).reshape(n_exp // te, te, d).transpose(0, 2, 1)

        proj = _in_projection(h, mod, lat_row, n1, w_in_l, gains, cos, sin_signed,
                              tiles_per_seq=tiles_per_seq, use_rope=True)
        proj_c = _in_projection(hc, mod, ctx_row_map, n1, w_in_l, gains, cos, sin_signed,
                                tiles_per_seq=tiles_per_seq, use_rope=False)
        ya = _global_attention(proj, proj_c, batch, seq, n_ctx)
        yb = _window_attention(sink_b[l], proj, proj_c, batch, seq, n_ctx)
        h = _out_projection(h, mod, lat_row, ya, yb, proj, ws, bs_b, gg, w_out_l, "out_projection")
        routed = _peer_route(h, mod, lat_row, n2, wq_t, keys, "peer_route")
        h = _peer_residual(h, mod, lat_row, _peer_dense(*routed, u_tab, v_tiles, "peer_dense"), "peer_residual")
        if not last:
            cya = _context_attention(no_sink, proj_c, batch, n_ctx, OFF_QA, OFF_KA, OFF_VA, "attn_ctx_a")
            cyb = _context_attention(sink_b[l], proj_c, batch, n_ctx, OFF_QB, OFF_KB, OFF_VB, "attn_ctx_b")
            hc = _out_projection(hc, mod, ctx_row_map, cya, cyb, proj_c, ws, bs_b, gg, w_out_l, "out_projection_ctx")
            routed_c = _peer_route(hc, mod, ctx_row_map, n2, wq_t, keys, "peer_route_ctx")
            hc = _peer_residual(hc, mod, ctx_row_map, _peer_dense(*routed_c, u_tab, v_tiles, "peer_dense_ctx"),
                                "peer_residual_ctx")
    return h.reshape(batch, seq, d)
```
